```python
import math
import jax, jax.numpy as jnp
from jax import lax
import numpy as np

D_MODEL = 1024
BATCH = 32
SEQ = 2048
DEPTH = 2
DEC_BATCH = 128
DEC_SEQ = 1
PAST_LEN = 16384
PAGE_SIZE = 128

N_META = 16
Q_BLOCK = 128
ROPE_THETA = 500000.0
LN_EPS = 1e-5
RMS_EPS = 1e-6
DEEPNORM_ALPHA = (2 * DEPTH) ** 0.25
DEEPNORM_BETA = (8 * DEPTH) ** -0.25
N_A = (DEPTH + 1) // 2
N_B = DEPTH // 2
H_A = 16
NOPE_A = 64
ROPE_A = 32
V_A = 64
Q_LORA = 512
KV_LORA = 256
SCALE_A = (NOPE_A + ROPE_A) ** -0.5
H_B = 8
KVH_B = 4
GROUP_B = H_B // KVH_B
DH_B = 64
ROT_B = DH_B // 4
SCALE_B = DH_B ** -0.5
D_FF = ((8 * D_MODEL + 767) // 768) * 256

kernel_name = 'hybrid_mla_diffattn_decoder_step'


def rmsnorm(x, g):
    xf = x.astype(jnp.float32)
    y = xf * lax.rsqrt(jnp.mean(xf * xf, axis=-1, keepdims=True) + RMS_EPS)
    return (y * g.astype(jnp.float32)).astype(x.dtype)


def layernorm(x, g, b):
    xf = x.astype(jnp.float32)
    mu = jnp.mean(xf, axis=-1, keepdims=True)
    var = jnp.mean(jnp.square(xf - mu), axis=-1, keepdims=True)
    y = (xf - mu) * lax.rsqrt(var + LN_EPS) * g.astype(jnp.float32) + b.astype(jnp.float32)
    return y.astype(x.dtype)


def post_norm(x, fx, g, b):
    return layernorm(DEEPNORM_ALPHA * x + fx, g, b)


def rope(x, pos):
    d = x.shape[-1]
    inv = jnp.power(ROPE_THETA, -jnp.arange(0, d, 2, dtype=jnp.float32) / d)
    ang = pos.astype(jnp.float32)[:, None] * inv[None, :]
    shape = (1, pos.shape[0]) + (1,) * (x.ndim - 3) + (d // 2,)
    cos = jnp.cos(ang).reshape(shape)
    sin = jnp.sin(ang).reshape(shape)
    xf = x.astype(jnp.float32)
    x1, x2 = xf[..., : d // 2], xf[..., d // 2:]
    return jnp.concatenate([x1 * cos - x2 * sin, x2 * cos + x1 * sin], axis=-1).astype(x.dtype)


def partial_rope(x, pos):
    return jnp.concatenate([rope(x[..., :ROT_B], pos), x[..., ROT_B:]], axis=-1)


def gather_pages(pool, page_table):
    g = pool[page_table]
    return g.reshape((g.shape[0], g.shape[1] * g.shape[2]) + g.shape[3:])


def causal_query_blocks(attend, length):
    bounds = [(0, N_META)] + [(s, min(s + Q_BLOCK, length)) for s in range(N_META, length, Q_BLOCK)]
    return jnp.concatenate([attend(s, e) for s, e in bounds], axis=1)


def mla_project(x, pos, w_in, g_q, g_kv, w_uq, w_uk):
    b, t, _ = x.shape
    c_q, c_kv, k_rope = jnp.split(x @ w_in, [Q_LORA, Q_LORA + KV_LORA], axis=-1)
    c_kv = rmsnorm(c_kv, g_kv)
    q = (rmsnorm(c_q, g_q) @ w_uq).reshape(b, t, H_A, NOPE_A + ROPE_A)
    q_lat = jnp.einsum('bthd,hrd->bthr', q[..., :NOPE_A], w_uk)
    return q_lat, rope(q[..., NOPE_A:], pos), c_kv, rope(k_rope, pos)


def mla_attend(q_lat, q_rope, ckv, krope, qpos, kpos):
    s = (jnp.einsum('bqhr,bkr->bhqk', q_lat, ckv).astype(jnp.float32)
         + jnp.einsum('bqhd,bkd->bhqk', q_rope, krope).astype(jnp.float32)) * SCALE_A
    s = jnp.where(qpos[:, None] >= kpos[None, :], s, -jnp.inf)
    p = jax.nn.softmax(s, axis=-1)
    return jnp.einsum('bhqk,bkr->bqhr', p.astype(ckv.dtype), ckv)


def mla_output(o_lat, w_uv, w_o):
    b, t = o_lat.shape[:2]
    o = jnp.einsum('bthr,hrv->bthv', o_lat, w_uv)
    return o.reshape(b, t, H_A * V_A) @ w_o


def diff_project(x, pos, w_in):
    b, t, _ = x.shape
    nq = H_B * 2 * DH_B
    nk = KVH_B * 2 * DH_B
    q, k, v = jnp.split(x @ w_in, [nq, nq + nk], axis=-1)
    q = partial_rope(q.reshape(b, t, H_B, 2, DH_B), pos)
    k = partial_rope(k.reshape(b, t, KVH_B, 2, DH_B), pos)
    return q, k.reshape(b, t, KVH_B, 2 * DH_B), v.reshape(b, t, KVH_B, 2 * DH_B)


def diff_lambda(lq1, lk1, lq2, lk2, lam_init):
    f32 = jnp.float32
    return (jnp.exp(jnp.sum(lq1.astype(f32) * lk1.astype(f32)))
            - jnp.exp(jnp.sum(lq2.astype(f32) * lk2.astype(f32))) + lam_init)


def diff_attend(q, k, v, qpos, kpos, lam):
    b, nq = q.shape[:2]
    qg = q.reshape(b, nq, KVH_B, GROUP_B, 2, DH_B)
    kk = k.reshape(k.shape[0], k.shape[1], KVH_B, 2, DH_B)
    s = jnp.einsum('bqngcd,bkncd->bngcqk', qg, kk).astype(jnp.float32) * SCALE_B
    s = jnp.where(qpos[:, None] >= kpos[None, :], s, -jnp.inf)
    p = jax.nn.softmax(s, axis=-1)
    a = p[:, :, :, 0] - lam * p[:, :, :, 1]
    o = jnp.einsum('bngqk,bknd->bqngd', a.astype(v.dtype), v)
    return o.reshape(b, nq, H_B, 2 * DH_B)


def diff_output(o, g_sub, lam_init, w_o):
    b, t = o.shape[:2]
    o = rmsnorm(o, g_sub) * (1.0 - lam_init)
    return o.reshape(b, t, H_B * 2 * DH_B) @ w_o


def swiglu(x, w_gu, w_down):
    gate, up = jnp.split(x @ w_gu, [D_FF], axis=-1)
    return (jax.nn.silu(gate) * up) @ w_down


def setup_inputs(seed: int = 0) -> dict:
    key = jax.random.key(seed)
    ks = iter(jax.random.split(key, 40))

    def nrm(shape, scale):
        return jax.random.normal(next(ks), shape, jnp.float32) * scale

    n_pages = PAST_LEN // PAGE_SIZE
    n_pool = (5 * DEC_BATCH * n_pages) // 4
    page_table = jax.random.permutation(next(ks), n_pool)[: DEC_BATCH * n_pages]
    page_table = page_table.reshape(DEC_BATCH, n_pages).astype(jnp.int32)
    d = D_MODEL
    beta = DEEPNORM_BETA
    return {
        'x_prompt': nrm((BATCH, SEQ, d), 1.0),
        'x_sample': nrm((DEC_BATCH, DEC_SEQ, d), 1.0),
        'cache_mla_ckv': nrm((N_A, n_pool, PAGE_SIZE, KV_LORA), 1.0),
        'cache_mla_krope': nrm((N_A, n_pool, PAGE_SIZE, ROPE_A), 1.0),
        'cache_diff_k': nrm((N_B, n_pool, PAGE_SIZE, KVH_B, 2 * DH_B), 1.0),
        'cache_diff_v': nrm((N_B, n_pool, PAGE_SIZE, KVH_B, 2 * DH_B), 1.0),
        'page_table': page_table,
        'meta_tokens': nrm((N_META, d), 1.0),
        'mla_w_in': nrm((N_A, d, Q_LORA + KV_LORA + ROPE_A), d ** -0.5),
        'mla_g_q': 1.0 + nrm((N_A, Q_LORA), 0.02),
        'mla_g_kv': 1.0 + nrm((N_A, KV_LORA), 0.02),
        'mla_w_uq': nrm((N_A, Q_LORA, H_A * (NOPE_A + ROPE_A)), Q_LORA ** -0.5),
        'mla_w_uk': nrm((N_A, H_A, KV_LORA, NOPE_A), KV_LORA ** -0.5),
        'mla_w_uv': nrm((N_A, H_A, KV_LORA, V_A), KV_LORA ** -0.5 * beta),
        'mla_w_o': nrm((N_A, H_A * V_A, d), (H_A * V_A) ** -0.5 * beta),
        'diff_w_in': jnp.concatenate([
            nrm((N_B, d, H_B * 2 * DH_B), d ** -0.5),
            nrm((N_B, d, KVH_B * 2 * DH_B), d ** -0.5),
            nrm((N_B, d, KVH_B * 2 * DH_B), d ** -0.5 * beta)], axis=-1),
        'diff_lam_q1': nrm((N_B, DH_B), 0.1),
        'diff_lam_k1': nrm((N_B, DH_B), 0.1),
        'diff_lam_q2': nrm((N_B, DH_B), 0.1),
        'diff_lam_k2': nrm((N_B, DH_B), 0.1),
        'diff_g_sub': 1.0 + nrm((N_B, 2 * DH_B), 0.02),
        'diff_w_o': nrm((N_B, H_B * 2 * DH_B, d), (H_B * 2 * DH_B) ** -0.5 * beta),
        'ffn_w_gu': jnp.concatenate([
            nrm((DEPTH, d, D_FF), d ** -0.5),
            nrm((DEPTH, d, D_FF), d ** -0.5 * beta)], axis=-1),
        'ffn_w_down': nrm((DEPTH, D_FF, d), D_FF ** -0.5 * beta),
        'ln1_g': 1.0 + nrm((DEPTH, d), 0.02),
        'ln1_b': nrm((DEPTH, d), 0.02),
        'ln2_g': 1.0 + nrm((DEPTH, d), 0.02),
        'ln2_b': nrm((DEPTH, d), 0.02),
    }


def reference(x_prompt, x_sample, cache_mla_ckv, cache_mla_krope, cache_diff_k, cache_diff_v,
              page_table, meta_tokens, mla_w_in, mla_g_q, mla_g_kv, mla_w_uq, mla_w_uk, mla_w_uv,
              mla_w_o, diff_w_in, diff_lam_q1, diff_lam_k1, diff_lam_q2, diff_lam_k2, diff_g_sub,
              diff_w_o, ffn_w_gu, ffn_w_down, ln1_g, ln1_b, ln2_g, ln2_b):
    bp, sp, _ = x_prompt.shape
    seq_p = N_META + sp
    dec_seq = x_sample.shape[1]
    past = page_table.shape[1] * PAGE_SIZE
    pos_p = jnp.arange(seq_p, dtype=jnp.int32)
    pos_s = past + jnp.arange(dec_seq, dtype=jnp.int32)
    kpos_s = jnp.arange(past + dec_seq, dtype=jnp.int32)

    meta = jnp.broadcast_to(meta_tokens.astype(x_prompt.dtype)[None], (bp, N_META, D_MODEL))
    hp = jnp.concatenate([meta, x_prompt], axis=1)
    hs = x_sample

    p_ckv, p_kr, p_k, p_v = [], [], [], []
    s_ckv, s_kr, s_k, s_v = [], [], [], []
    for i in range(DEPTH):
        j = i // 2
        if i % 2 == 0:
            wts = (mla_w_in[j], mla_g_q[j], mla_g_kv[j], mla_w_uq[j], mla_w_uk[j])
            ql_p, qr_p, c_p, r_p = mla_project(hp, pos_p, *wts)
            o_p = causal_query_blocks(
                lambda s, e: mla_attend(ql_p[:, s:e], qr_p[:, s:e], c_p[:, :e], r_p[:, :e],
                                        pos_p[s:e], pos_p[:e]), seq_p)
            ql_s, qr_s, c_s, r_s = mla_project(hs, pos_s, *wts)
            c_past = gather_pages(cache_mla_ckv[j], page_table)
            r_past = gather_pages(cache_mla_krope[j], page_table)
            c_all = jnp.concatenate([c_past, c_s.astype(c_past.dtype)], axis=1)
            r_all = jnp.concatenate([r_past, r_s.astype(r_past.dtype)], axis=1)
            o_s = mla_attend(ql_s, qr_s, c_all, r_all, pos_s, kpos_s)
            mix_p = mla_output(o_p, mla_w_uv[j], mla_w_o[j])
            mix_s = mla_output(o_s, mla_w_uv[j], mla_w_o[j])
            p_ckv.append(c_p)
            p_kr.append(r_p)
            s_ckv.append(c_s)
            s_kr.append(r_s)
        else:
            lam_init = 0.8 - 0.6 * math.exp(-0.3 * i)
            lam = diff_lambda(diff_lam_q1[j], diff_lam_k1[j], diff_lam_q2[j], diff_lam_k2[j], lam_init)
            q_p, k_p, v_p = diff_project(hp, pos_p, diff_w_in[j])
            o_p = causal_query_blocks(
                lambda s, e: diff_attend(q_p[:, s:e], k_p[:, :e], v_p[:, :e],
                                         pos_p[s:e], pos_p[:e], lam), seq_p)
            q_s, k_s, v_s = diff_project(hs, pos_s, diff_w_in[j])
            k_past = gather_pages(cache_diff_k[j], page_table)
            v_past = gather_pages(cache_diff_v[j], page_table)
            k_all = jnp.concatenate([k_past, k_s.astype(k_past.dtype)], axis=1)
            v_all = jnp.concatenate([v_past, v_s.astype(v_past.dtype)], axis=1)
            o_s = diff_attend(q_s, k_all, v_all, pos_s, kpos_s, lam)
            mix_p = diff_output(o_p, diff_g_sub[j], lam_init, diff_w_o[j])
            mix_s = diff_output(o_s, diff_g_sub[j], lam_init, diff_w_o[j])
            p_k.append(k_p)
            p_v.append(v_p)
            s_k.append(k_s)
            s_v.append(v_s)
        hp = post_norm(hp, mix_p, ln1_g[i], ln1_b[i])
        hs = post_norm(hs, mix_s, ln1_g[i], ln1_b[i])
        hp = post_norm(hp, swiglu(hp, ffn_w_gu[i], ffn_w_down[i]), ln2_g[i], ln2_b[i])
        hs = post_norm(hs, swiglu(hs, ffn_w_gu[i], ffn_w_down[i]), ln2_g[i], ln2_b[i])

    y_prompt = hp[:, N_META:]
    y_sample = hs
    new_prompt_ckv = jnp.stack(p_ckv, axis=0)
    new_prompt_krope = jnp.stack(p_kr, axis=0)
    new_prompt_k = jnp.stack(p_k, axis=0)
    new_prompt_v = jnp.stack(p_v, axis=0)
    new_sample_ckv = jnp.stack(s_ckv, axis=0)
    new_sample_krope = jnp.stack(s_kr, axis=0)
    new_sample_k = jnp.stack(s_k, axis=0)
    new_sample_v = jnp.stack(s_v, axis=0)
    return (y_prompt, y_sample, new_prompt_ckv, new_prompt_krope, new_prompt_k, new_prompt_v,
            new_sample_ckv, new_sample_krope, new_sample_k, new_sample_v)
```

```python
import functools
import math

import jax
import jax.numpy as jnp
from jax import lax
from jax.experimental import pallas as pl
from jax.experimental.pallas import tpu as pltpu

F32 = jnp.float32
BF16 = jnp.bfloat16

N_META = 16
PAGE = 128
ROPE_THETA = 500000.0
LN_EPS = 1e-5
RMS_EPS = 1e-6
H_A, NOPE_A, ROPE_A, V_A = 16, 64, 32, 64
Q_LORA, KV_LORA = 512, 256
SCALE_A = (NOPE_A + ROPE_A) ** -0.5
H_B, KVH_B, DH_B = 8, 4, 64
GROUP_B = H_B // KVH_B
ROT_B = DH_B // 4
SCALE_B = DH_B ** -0.5
LANES = 128
KA_PAD = KV_LORA + LANES

TM = 512
TQ, TK = 256, 512
FF_CHUNK = 256
MLA_PAGES_PER_CHUNK = 16
DIFF_PAGES_PER_CHUNK = 8
DEC_NBUF = 3
MIB = 1024 * 1024


def _params(sem, vmem_mib):
    return pltpu.CompilerParams(dimension_semantics=sem, vmem_limit_bytes=vmem_mib * MIB)


def _full(shape):
    nd = len(shape)
    return pl.BlockSpec(shape, lambda *_: (0,) * nd)


def _dot(a, b):
    return jnp.dot(a, b, preferred_element_type=F32)


def _dot_nt(a, b):
    return lax.dot_general(a, b, (((1,), (1,)), ((), ())), preferred_element_type=F32)


def _rms(x, g):
    return x * lax.rsqrt(jnp.mean(x * x, axis=-1, keepdims=True) + RMS_EPS) * g


def _layernorm(x, g, b):
    mu = jnp.mean(x, axis=-1, keepdims=True)
    xc = x - mu
    var = jnp.mean(xc * xc, axis=-1, keepdims=True)
    return xc * lax.rsqrt(var + LN_EPS) * g + b


def _rot_half(x, half):
    lane = lax.broadcasted_iota(jnp.int32, x.shape, 1)
    first = (lane % (2 * half)) < half
    return jnp.where(first, -pltpu.roll(x, LANES - half, 1), pltpu.roll(x, half, 1))


def _mla_in_kernel(x_ref, w_ref, gq_ref, gkv_ref, cos_ref, sin_ref, cq_ref, ckv_ref, kr_ref, kb_ref):
    y = _dot(x_ref[...].astype(BF16), w_ref[...])
    cq_ref[...] = _rms(y[:, :Q_LORA], gq_ref[...]).astype(BF16)
    ckv = _rms(y[:, Q_LORA:Q_LORA + KV_LORA], gkv_ref[...])
    ckv_ref[...] = ckv
    kr = y[:, Q_LORA + KV_LORA:]
    kr = kr * cos_ref[...] + _rot_half(kr, ROPE_A // 2) * sin_ref[...]
    kr_ref[...] = kr[:, :ROPE_A]
    kb_ref[...] = jnp.concatenate([ckv, kr], axis=1).astype(BF16)


def _mla_in(x, w, gq, gkv, cos, sin):
    m, d = x.shape
    tm = min(TM, m)
    nt = cos.shape[0] // tm
    row = lambda i: (i, 0)
    tab = lambda i: (i % nt, 0)
    return pl.pallas_call(
        _mla_in_kernel,
        grid=(m // tm,),
        in_specs=[pl.BlockSpec((tm, d), row), _full(w.shape), _full(gq.shape), _full(gkv.shape),
                  pl.BlockSpec((tm, LANES), tab), pl.BlockSpec((tm, LANES), tab)],
        out_specs=[pl.BlockSpec((tm, Q_LORA), row), pl.BlockSpec((tm, KV_LORA), row),
                   pl.BlockSpec((tm, ROPE_A), row), pl.BlockSpec((tm, KA_PAD), row)],
        out_shape=[jax.ShapeDtypeStruct((m, Q_LORA), BF16), jax.ShapeDtypeStruct((m, KV_LORA), F32),
                   jax.ShapeDtypeStruct((m, ROPE_A), F32), jax.ShapeDtypeStruct((m, KA_PAD), BF16)],
        compiler_params=_params(("parallel",), 32),
        name="mla_in",
    )(x, w, gq, gkv, cos, sin)


def _mla_q_kernel(cq_ref, wuq_ref, wuk_ref, cos_ref, sin_ref, q_ref):
    y = _dot(cq_ref[...], wuq_ref[...])
    cos, sin = cos_ref[...], sin_ref[...]
    lane = lax.broadcasted_iota(jnp.int32, cos.shape, 1)
    heads_per_slab = LANES // ROPE_A
    for slab in range(H_A // heads_per_slab):
        base = H_A * NOPE_A + slab * LANES
        sl = y[:, base:base + LANES]
        ro = (sl * cos + _rot_half(sl, ROPE_A // 2) * sin) * SCALE_A
        for k in range(heads_per_slab):
            h = slab * heads_per_slab + k
            piece = ro if k == 0 else pltpu.roll(ro, LANES - ROPE_A * k, 1)
            piece = jnp.where(lane < ROPE_A, piece, 0.0)
            qlat = _dot(y[:, h * NOPE_A:(h + 1) * NOPE_A].astype(BF16), wuk_ref[h]) * SCALE_A
            q_ref[h] = jnp.concatenate([qlat, piece], axis=1).astype(BF16)


def _mla_q(cq, wuq, wuk, cos, sin):
    m = cq.shape[0]
    tm = min(TM, m)
    nt = cos.shape[0] // tm
    tab = lambda i: (i % nt, 0)
    return pl.pallas_call(
        _mla_q_kernel,
        grid=(m // tm,),
        in_specs=[pl.BlockSpec((tm, Q_LORA), lambda i: (i, 0)), _full(wuq.shape), _full(wuk.shape),
                  pl.BlockSpec((tm, LANES), tab), pl.BlockSpec((tm, LANES), tab)],
        out_specs=pl.BlockSpec((H_A, tm, KA_PAD), lambda i: (0, i, 0)),
        out_shape=jax.ShapeDtypeStruct((H_A, m, KA_PAD), BF16),
        compiler_params=_params(("parallel",), 40),
        name="mla_q",
    )(cq, wuq, wuk, cos, sin)


def _causal_mask(qi, ki, tq, tk):
    r = qi * tq + lax.broadcasted_iota(jnp.int32, (tq, tk), 0)
    c = ki * tk + lax.broadcasted_iota(jnp.int32, (tq, tk), 1)
    return r >= c


def _mla_attn_kernel(*refs, tq, tk, nk, has_prefix):
    if has_prefix:
        q_ref, k_ref, km_ref, wuv_ref, o_ref, m_sc, l_sc, acc_sc = refs
    else:
        q_ref, k_ref, wuv_ref, o_ref, m_sc, l_sc, acc_sc = refs
    qi, ki = pl.program_id(1), pl.program_id(2)

    @pl.when(ki == 0)
    def _init():
        if has_prefix:
            km = km_ref[...]

            def head(h, c):
                s = _dot_nt(q_ref[h], km)
                m = jnp.max(s, axis=-1, keepdims=True)
                p = jnp.exp(s - m)
                m_sc[h] = m
                l_sc[h] = jnp.sum(p, axis=-1, keepdims=True)
                acc_sc[h] = _dot(p.astype(BF16), km[:, :KV_LORA])
                return c

            lax.fori_loop(0, H_A, head, 0)
        else:
            m_sc[...] = jnp.full(m_sc.shape, -jnp.inf, F32)
            l_sc[...] = jnp.zeros(l_sc.shape, F32)
            acc_sc[...] = jnp.zeros(acc_sc.shape, F32)

    @pl.when(ki * tk <= qi * tq + tq - 1)
    def _step():
        k = k_ref[...]
        v = k[:, :KV_LORA]
        mask = _causal_mask(qi, ki, tq, tk)

        def head(h, c):
            s = jnp.where(mask, _dot_nt(q_ref[h], k), -jnp.inf)
            m_prev = m_sc[h]
            m_new = jnp.maximum(m_prev, jnp.max(s, axis=-1, keepdims=True))
            a = jnp.exp(m_prev - m_new)
            p = jnp.exp(s - m_new)
            l_sc[h] = a * l_sc[h] + jnp.sum(p, axis=-1, keepdims=True)
            acc_sc[h] = a * acc_sc[h] + _dot(p.astype(BF16), v)
            m_sc[h] = m_new
            return c

        lax.fori_loop(0, H_A, head, 0)

    @pl.when(ki == nk - 1)
    def _fin():
        outs = [_dot((acc_sc[h] / l_sc[h]).astype(BF16), wuv_ref[h]) for h in range(H_A)]
        o_ref[...] = jnp.concatenate(outs, axis=1).astype(BF16)


def _mla_attn(q, k, kmeta, wuv, nb, s):
    tq, tk = min(TQ, s), min(TK, s)
    nq, nk = s // tq, s // tk
    has_prefix = kmeta is not None
    kblk = lambda b, qi, ki: (b * nk + jnp.minimum(ki, (qi * tq + tq - 1) // tk), 0)
    in_specs = [pl.BlockSpec((H_A, tq, KA_PAD), lambda b, qi, ki: (0, b * nq + qi, 0)),
                pl.BlockSpec((tk, KA_PAD), kblk)]
    args = [q, k]
    if has_prefix:
        in_specs.append(_full(kmeta.shape))
        args.append(kmeta)
    in_specs.append(_full(wuv.shape))
    args.append(wuv)
    return pl.pallas_call(
        functools.partial(_mla_attn_kernel, tq=tq, tk=tk, nk=nk, has_prefix=has_prefix),
        grid=(nb, nq, nk),
        in_specs=in_specs,
        out_specs=pl.BlockSpec((tq, H_A * V_A), lambda b, qi, ki: (b * nq + qi, 0)),
        out_shape=jax.ShapeDtypeStruct((nb * s, H_A * V_A), BF16),
        scratch_shapes=[pltpu.VMEM((H_A, tq, 1), F32), pltpu.VMEM((H_A, tq, 1), F32),
                        pltpu.VMEM((H_A, tq, KV_LORA), F32)],
        compiler_params=_params(("parallel", "parallel", "arbitrary"), 40),
        name="mla_attn",
    )(*args)


def _diff_lambda(lam_ref, lam_init):
    v = lam_ref[...]
    s1 = jnp.sum(v[0:1] * v[1:2], axis=-1, keepdims=True)
    s2 = jnp.sum(v[2:3] * v[3:4], axis=-1, keepdims=True)
    return jnp.exp(s1) - jnp.exp(s2) + lam_init


def _diff_attn_kernel(*refs, tq, tk, nk, has_prefix, lam_init):
    if has_prefix:
        q_ref, k_ref, v_ref, km_ref, vm_ref, lam_ref, g_ref, o_ref, m_sc, l_sc, acc_sc = refs
    else:
        q_ref, k_ref, v_ref, lam_ref, g_ref, o_ref, m_sc, l_sc, acc_sc = refs
    qi, ki = pl.program_id(1), pl.program_id(2)
    hd = 2 * DH_B
    lane = lax.broadcasted_iota(jnp.int32, (tq, hd), 1)

    def q_maps(h):
        qh = q_ref[:, h * hd:(h + 1) * hd]
        zero = jnp.zeros_like(qh)
        return jnp.where(lane < DH_B, qh, zero), jnp.where(lane >= DH_B, qh, zero)

    @pl.when(ki == 0)
    def _init():
        if has_prefix:
            for n in range(KVH_B):
                kn = km_ref[:, n * hd:(n + 1) * hd]
                vn = vm_ref[:, n * hd:(n + 1) * hd]
                for g in range(GROUP_B):
                    h = n * GROUP_B + g
                    for c, qc in enumerate(q_maps(h)):
                        s = _dot_nt(qc, kn)
                        m = jnp.max(s, axis=-1, keepdims=True)
                        p = jnp.exp(s - m)
                        m_sc[2 * h + c] = m
                        l_sc[2 * h + c] = jnp.sum(p, axis=-1, keepdims=True)
                        acc_sc[2 * h + c] = _dot(p.astype(BF16), vn)
        else:
            m_sc[...] = jnp.full(m_sc.shape, -jnp.inf, F32)
            l_sc[...] = jnp.zeros(l_sc.shape, F32)
            acc_sc[...] = jnp.zeros(acc_sc.shape, F32)

    @pl.when(ki * tk <= qi * tq + tq - 1)
    def _step():
        mask = _causal_mask(qi, ki, tq, tk)
        for n in range(KVH_B):
            kn = k_ref[:, n * hd:(n + 1) * hd]
            vn = v_ref[:, n * hd:(n + 1) * hd]
            for g in range(GROUP_B):
                h = n * GROUP_B + g
                for c, qc in enumerate(q_maps(h)):
                    i = 2 * h + c
                    s = jnp.where(mask, _dot_nt(qc, kn), -jnp.inf)
                    m_prev = m_sc[i]
                    m_new = jnp.maximum(m_prev, jnp.max(s, axis=-1, keepdims=True))
                    a = jnp.exp(m_prev - m_new)
                    p = jnp.exp(s - m_new)
                    l_sc[i] = a * l_sc[i] + jnp.sum(p, axis=-1, keepdims=True)
                    acc_sc[i] = a * acc_sc[i] + _dot(p.astype(BF16), vn)
                    m_sc[i] = m_new

    @pl.when(ki == nk - 1)
    def _fin():
        lam = _diff_lambda(lam_ref, lam_init)
        for h in range(H_B):
            o = acc_sc[2 * h] / l_sc[2 * h] - lam * (acc_sc[2 * h + 1] / l_sc[2 * h + 1])
            o_ref[:, h * hd:(h + 1) * hd] = (_rms(o, g_ref[...]) * (1.0 - lam_init)).astype(BF16)


def _diff_attn(q, k, v, kmeta, vmeta, lam_vecs, g_sub, lam_init, nb, s):
    tq, tk = min(TQ, s), min(TK, s)
    nq, nk = s // tq, s // tk
    has_prefix = kmeta is not None
    kw = KVH_B * 2 * DH_B
    kblk = lambda b, qi, ki: (b * nk + jnp.minimum(ki, (qi * tq + tq - 1) // tk), 0)
    in_specs = [pl.BlockSpec((tq, H_B * 2 * DH_B), lambda b, qi, ki: (b * nq + qi, 0)),
                pl.BlockSpec((tk, kw), kblk), pl.BlockSpec((tk, kw), kblk)]
    args = [q, k, v]
    if has_prefix:
        in_specs += [_full(kmeta.shape), _full(vmeta.shape)]
        args += [kmeta, vmeta]
    in_specs += [_full(lam_vecs.shape), _full(g_sub.shape)]
    args += [lam_vecs, g_sub]
    nvh = 2 * H_B
    return pl.pallas_call(
        functools.partial(_diff_attn_kernel, tq=tq, tk=tk, nk=nk, has_prefix=has_prefix, lam_init=lam_init),
        grid=(nb, nq, nk),
        in_specs=in_specs,
        out_specs=pl.BlockSpec((tq, H_B * 2 * DH_B), lambda b, qi, ki: (b * nq + qi, 0)),
        out_shape=jax.ShapeDtypeStruct((nb * s, H_B * 2 * DH_B), BF16),
        scratch_shapes=[pltpu.VMEM((nvh, tq, 1), F32), pltpu.VMEM((nvh, tq, 1), F32),
                        pltpu.VMEM((nvh, tq, 2 * DH_B), F32)],
        compiler_params=_params(("parallel", "parallel", "arbitrary"), 40),
        name="diff_attn",
    )(*args)


def _chunk_pipeline(n_chunks_total, start_chunk, wait_chunk, b, per_batch, consume, carry):
    g0 = b * per_batch

    @pl.when(b == 0)
    def _prime():
        for d in range(DEC_NBUF - 1):
            if d < n_chunks_total:
                start_chunk(d, d % DEC_NBUF)

    def body(c, carry):
        g = g0 + c
        slot = g % DEC_NBUF
        wait_chunk(slot)
        nxt = g + DEC_NBUF - 1

        @pl.when(nxt < n_chunks_total)
        def _():
            start_chunk(nxt, nxt % DEC_NBUF)

        return consume(slot, carry)

    return lax.fori_loop(0, per_batch, body, carry)


def _mla_decode_kernel(pt_ref, q_ref, knew_ref, ckv_hbm, kr_hbm, o_ref, ckv_buf, kr_buf, sem_c, sem_r,
                       *, layer, n_pages, cpp, n_batch):
    b = pl.program_id(0)
    per_batch = n_pages // cpp
    total = n_batch * per_batch

    def copies(g, slot, j):
        page = pt_ref[g * cpp + j]
        return (pltpu.make_async_copy(ckv_hbm.at[layer, page], ckv_buf.at[slot, j], sem_c.at[slot]),
                pltpu.make_async_copy(kr_hbm.at[layer, page], kr_buf.at[slot, j], sem_r.at[slot]))

    def start_chunk(g, slot):
        for j in range(cpp):
            for cp in copies(g, slot, j):
                cp.start()

    def wait_chunk(slot):
        for j in range(cpp):
            for cp in copies(0, slot, j):
                cp.wait()

    q = q_ref[0]
    qlat, qr = q[:, :KV_LORA], q[:, KV_LORA:KV_LORA + ROPE_A]
    knew = knew_ref[0].astype(F32)
    m0 = jnp.sum(q.astype(F32) * knew, axis=-1, keepdims=True)
    l0 = jnp.ones_like(m0)
    acc0 = jnp.broadcast_to(knew[:, :KV_LORA], (H_A, KV_LORA))

    def consume(slot, carry):
        m_prev, l_prev, acc = carry
        kc = ckv_buf[slot].reshape(cpp * PAGE, KV_LORA).astype(BF16)
        kr = kr_buf[slot].reshape(cpp * PAGE, ROPE_A).astype(BF16)
        s = _dot_nt(qlat, kc) + _dot_nt(qr, kr)
        m_new = jnp.maximum(m_prev, jnp.max(s, axis=-1, keepdims=True))
        a = jnp.exp(m_prev - m_new)
        p = jnp.exp(s - m_new)
        l_new = a * l_prev + jnp.sum(p, axis=-1, keepdims=True)
        return m_new, l_new, a * acc + _dot(p.astype(BF16), kc)

    _, l, acc = _chunk_pipeline(total, start_chunk, wait_chunk, b, per_batch, consume, (m0, l0, acc0))
    o_ref[0] = acc / l


def _mla_decode(pt_flat, q, knew, cache_ckv, cache_kr, layer, n_pages):
    nb = q.shape[0]
    cpp = min(MLA_PAGES_PER_CHUNK, n_pages)
    grid_spec = pltpu.PrefetchScalarGridSpec(
        num_scalar_prefetch=1,
        grid=(nb,),
        in_specs=[pl.BlockSpec((1, H_A, KA_PAD), lambda b, pt: (b, 0, 0)),
                  pl.BlockSpec((1, 1, KA_PAD), lambda b, pt: (b, 0, 0)),
                  pl.BlockSpec(memory_space=pl.ANY), pl.BlockSpec(memory_space=pl.ANY)],
        out_specs=pl.BlockSpec((1, H_A, KV_LORA), lambda b, pt: (b, 0, 0)),
        scratch_shapes=[pltpu.VMEM((DEC_NBUF, cpp, PAGE, KV_LORA), F32),
                        pltpu.VMEM((DEC_NBUF, cpp, PAGE, ROPE_A), F32),
                        pltpu.SemaphoreType.DMA((DEC_NBUF,)), pltpu.SemaphoreType.DMA((DEC_NBUF,))],
    )
    return pl.pallas_call(
        functools.partial(_mla_decode_kernel, layer=layer, n_pages=n_pages, cpp=cpp, n_batch=nb),
        grid_spec=grid_spec,
        out_shape=jax.ShapeDtypeStruct((nb, H_A, KV_LORA), F32),
        compiler_params=_params(("arbitrary",), 32),
        name="mla_decode",
    )(pt_flat, q, knew, cache_ckv, cache_kr)


def _mla_uv_kernel(o_ref, wuv_ref, out_ref):
    outs = [_dot(o_ref[h].astype(BF16), wuv_ref[h]) for h in range(H_A)]
    out_ref[...] = jnp.concatenate(outs, axis=1).astype(BF16)


def _mla_uv(olat, wuv):
    m = olat.shape[1]
    return pl.pallas_call(
        _mla_uv_kernel,
        in_specs=[_full(olat.shape), _full(wuv.shape)],
        out_specs=_full((m, H_A * V_A)),
        out_shape=jax.ShapeDtypeStruct((m, H_A * V_A), BF16),
        name="mla_uv",
    )(olat, wuv)


def _diff_decode_kernel(pt_ref, q_ref, knew_ref, vnew_ref, lam_ref, g_ref, k_hbm, v_hbm, o_ref,
                        k_buf, v_buf, sem_k, sem_v, *, layer, n_pages, cpp, n_batch, lam_init):
    b = pl.program_id(0)
    per_batch = n_pages // cpp
    total = n_batch * per_batch
    hd = 2 * DH_B
    prow = PAGE * KVH_B
    rows = 2 * GROUP_B

    def copies(g, slot, j):
        page = pt_ref[g * cpp + j]
        dst = pl.ds(j * prow, prow)
        return (pltpu.make_async_copy(k_hbm.at[layer, page], k_buf.at[slot, dst], sem_k.at[slot]),
                pltpu.make_async_copy(v_hbm.at[layer, page], v_buf.at[slot, dst], sem_v.at[slot]))

    def start_chunk(g, slot):
        for j in range(cpp):
            for cp in copies(g, slot, j):
                cp.start()

    def wait_chunk(slot):
        for j in range(cpp):
            for cp in copies(0, slot, j):
                cp.wait()

    qs = [q_ref[0, n] for n in range(KVH_B)]
    knew = knew_ref[0].astype(F32)
    vnew = vnew_ref[0].astype(F32)
    carry = []
    for n in range(KVH_B):
        m0 = jnp.sum(qs[n].astype(F32) * knew[:, n * hd:(n + 1) * hd], axis=-1, keepdims=True)
        carry += [m0, jnp.ones_like(m0), jnp.broadcast_to(vnew[:, n * hd:(n + 1) * hd], (8, hd))]

    def consume(slot, carry):
        out = []
        for n in range(KVH_B):
            m_prev, l_prev, acc = carry[3 * n:3 * n + 3]
            sel = pl.ds(n, cpp * PAGE, stride=KVH_B)
            kn = k_buf[slot, sel, :].astype(BF16)
            vn = v_buf[slot, sel, :].astype(BF16)
            s = _dot_nt(qs[n], kn)
            m_new = jnp.maximum(m_prev, jnp.max(s, axis=-1, keepdims=True))
            a = jnp.exp(m_prev - m_new)
            p = jnp.exp(s - m_new)
            out += [m_new, a * l_prev + jnp.sum(p, axis=-1, keepdims=True), a * acc + _dot(p.astype(BF16), vn)]
        return tuple(out)

    carry = _chunk_pipeline(total, start_chunk, wait_chunk, b, per_batch, consume, tuple(carry))
    lam = _diff_lambda(lam_ref, lam_init)
    heads = []
    for n in range(KVH_B):
        o = carry[3 * n + 2] / carry[3 * n + 1]
        heads.append(o[0:GROUP_B] - lam * o[GROUP_B:rows])
    o = jnp.concatenate(heads, axis=0)
    o_ref[0] = _rms(o, g_ref[...]) * (1.0 - lam_init)


def _diff_decode(pt_flat, q, knew, vnew, lam_vecs, g_sub, cache_k, cache_v, layer, n_pages, lam_init):
    nb = q.shape[0]
    cpp = min(DIFF_PAGES_PER_CHUNK, n_pages)
    hd = 2 * DH_B
    kw = KVH_B * hd
    grid_spec = pltpu.PrefetchScalarGridSpec(
        num_scalar_prefetch=1,
        grid=(nb,),
        in_specs=[pl.BlockSpec((1, KVH_B, 8, hd), lambda b, pt: (b, 0, 0, 0)),
                  pl.BlockSpec((1, 1, kw), lambda b, pt: (b, 0, 0)),
                  pl.BlockSpec((1, 1, kw), lambda b, pt: (b, 0, 0)),
                  pl.BlockSpec(lam_vecs.shape, lambda b, pt: (0, 0)),
                  pl.BlockSpec(g_sub.shape, lambda b, pt: (0, 0)),
                  pl.BlockSpec(memory_space=pl.ANY), pl.BlockSpec(memory_space=pl.ANY)],
        out_specs=pl.BlockSpec((1, H_B, hd), lambda b, pt: (b, 0, 0)),
        scratch_shapes=[pltpu.VMEM((DEC_NBUF, cpp * PAGE * KVH_B, hd), F32),
                        pltpu.VMEM((DEC_NBUF, cpp * PAGE * KVH_B, hd), F32),
                        pltpu.SemaphoreType.DMA((DEC_NBUF,)), pltpu.SemaphoreType.DMA((DEC_NBUF,))],
    )
    return pl.pallas_call(
        functools.partial(_diff_decode_kernel, layer=layer, n_pages=n_pages, cpp=cpp, n_batch=nb,
                          lam_init=lam_init),
        grid_spec=grid_spec,
        out_shape=jax.ShapeDtypeStruct((nb, H_B, hd), F32),
        compiler_params=_params(("arbitrary",), 40),
        name="diff_decode",
    )(pt_flat, q, knew, vnew, lam_vecs, g_sub, cache_k, cache_v)


def _diff_in_kernel(x_ref, w_ref, cos_ref, sin_ref, q_ref, k_ref, v_ref, kb_ref, vb_ref):
    y = _dot(x_ref[...].astype(BF16), w_ref[...])
    cos, sin = cos_ref[...], sin_ref[...]
    nq, nk = H_B * 2 * DH_B, KVH_B * 2 * DH_B

    def roped(lo, hi):
        parts = []
        for c0 in range(lo, hi, LANES):
            sl = y[:, c0:c0 + LANES]
            parts.append(sl * cos + _rot_half(sl, ROT_B // 2) * sin)
        return jnp.concatenate(parts, axis=1)

    q_ref[...] = (roped(0, nq) * SCALE_B).astype(BF16)
    k = roped(nq, nq + nk)
    k_ref[...] = k
    kb_ref[...] = k.astype(BF16)
    v = y[:, nq + nk:]
    v_ref[...] = v
    vb_ref[...] = v.astype(BF16)


def _diff_in(x, w, cos, sin):
    m, d = x.shape
    tm = min(TM, m)
    nt = cos.shape[0] // tm
    nq, nk = H_B * 2 * DH_B, KVH_B * 2 * DH_B
    row = lambda i: (i, 0)
    tab = lambda i: (i % nt, 0)
    return pl.pallas_call(
        _diff_in_kernel,
        grid=(m // tm,),
        in_specs=[pl.BlockSpec((tm, d), row), _full(w.shape),
                  pl.BlockSpec((tm, LANES), tab), pl.BlockSpec((tm, LANES), tab)],
        out_specs=[pl.BlockSpec((tm, nq), row), pl.BlockSpec((tm, nk), row), pl.BlockSpec((tm, nk), row),
                   pl.BlockSpec((tm, nk), row), pl.BlockSpec((tm, nk), row)],
        out_shape=[jax.ShapeDtypeStruct((m, nq), BF16), jax.ShapeDtypeStruct((m, nk), F32),
                   jax.ShapeDtypeStruct((m, nk), F32), jax.ShapeDtypeStruct((m, nk), BF16),
                   jax.ShapeDtypeStruct((m, nk), BF16)],
        compiler_params=_params(("parallel",), 40),
        name="diff_in",
    )(x, w, cos, sin)


def _proj_ln_kernel(o_ref, w_ref, x_ref, g_ref, b_ref, out_ref, *, alpha):
    mix = _dot(o_ref[...], w_ref[...])
    out_ref[...] = _layernorm(alpha * x_ref[...] + mix, g_ref[...], b_ref[...])


def _proj_ln(o, w, x, g, b, alpha):
    m, d = x.shape
    tm = min(TM, m)
    row = lambda i: (i, 0)
    return pl.pallas_call(
        functools.partial(_proj_ln_kernel, alpha=alpha),
        grid=(m // tm,),
        in_specs=[pl.BlockSpec((tm, o.shape[1]), row), _full(w.shape), pl.BlockSpec((tm, d), row),
                  _full(g.shape), _full(b.shape)],
        out_specs=pl.BlockSpec((tm, d), row),
        out_shape=jax.ShapeDtypeStruct((m, d), F32),
        compiler_params=_params(("parallel",), 32),
        name="proj_ln",
    )(o, w, x, g, b)


def _ffn_kernel(x_ref, wg_ref, wu_ref, wd_ref, g_ref, b_ref, out_ref, acc_ref, *, alpha, d_ff):
    x = x_ref[...]
    xb = x.astype(BF16)
    for i, c0 in enumerate(range(0, d_ff, FF_CHUNK)):
        gate = _dot(xb, wg_ref[:, c0:c0 + FF_CHUNK])
        up = _dot(xb, wu_ref[:, c0:c0 + FF_CHUNK])
        act = (gate * jax.nn.sigmoid(gate) * up).astype(BF16)
        part = _dot(act, wd_ref[c0:c0 + FF_CHUNK, :])
        if i == 0:
            acc_ref[...] = part
        else:
            acc_ref[...] += part
    out_ref[...] = _layernorm(alpha * x + acc_ref[...], g_ref[...], b_ref[...])


def _ffn(x, wg, wu, wd, g, b, alpha):
    m, d = x.shape
    d_ff = wg.shape[1]
    tm = min(TM, m)
    row = lambda i: (i, 0)
    resident = lambda shape: pl.BlockSpec(shape, lambda i: (0, 0), pipeline_mode=pl.Buffered(1))
    return pl.pallas_call(
        functools.partial(_ffn_kernel, alpha=alpha, d_ff=d_ff),
        grid=(m // tm,),
        in_specs=[pl.BlockSpec((tm, d), row), resident(wg.shape), resident(wu.shape), resident(wd.shape),
                  _full(g.shape), _full(b.shape)],
        out_specs=pl.BlockSpec((tm, d), row),
        out_shape=jax.ShapeDtypeStruct((m, d), F32),
        scratch_shapes=[pltpu.VMEM((tm, d), F32)],
        compiler_params=_params(("parallel",), 48),
        name="ffn",
    )(x, wg, wu, wd, g, b)


def _rope_tables(pos, d, period):
    inv = jnp.power(ROPE_THETA, -jnp.arange(0, d, 2, dtype=F32) / d)
    ang = pos.astype(F32)[:, None] * inv[None, :]
    cos, sin = jnp.cos(ang), jnp.sin(ang)
    n = pos.shape[0]
    cos = jnp.concatenate([cos, cos, jnp.ones((n, period - d), F32)], axis=1)
    sin = jnp.concatenate([sin, sin, jnp.zeros((n, period - d), F32)], axis=1)
    reps = LANES // period
    return jnp.tile(cos, (1, reps)), jnp.tile(sin, (1, reps))


def kernel(x_prompt, x_sample, cache_mla_ckv, cache_mla_krope, cache_diff_k, cache_diff_v, page_table, meta_tokens, mla_w_in, mla_g_q, mla_g_kv, mla_w_uq, mla_w_uk, mla_w_uv, mla_w_o, diff_w_in, diff_lam_q1, diff_lam_k1, diff_lam_q2, diff_lam_k2, diff_g_sub, diff_w_o, ffn_w_gu, ffn_w_down, ln1_g, ln1_b, ln2_g, ln2_b):
    nb, seq, d = x_prompt.shape
    ndec, dec_seq, _ = x_sample.shape
    assert dec_seq == 1 and meta_tokens.shape[0] == N_META
    n_pages = page_table.shape[1]
    past = n_pages * PAGE
    depth = ln1_g.shape[0]
    alpha = (2 * depth) ** 0.25
    d_ff = ffn_w_down.shape[1]
    hd = 2 * DH_B

    hr = x_prompt.reshape(nb * seq, d)
    hs = jnp.concatenate([meta_tokens.astype(F32), x_sample.reshape(ndec, d)], axis=0)
    pos_r = N_META + jnp.arange(seq, dtype=jnp.int32)
    pos_s = jnp.concatenate([jnp.arange(N_META, dtype=jnp.int32), jnp.full((ndec,), past, jnp.int32)])
    pt_flat = page_table.reshape(-1).astype(jnp.int32)
    row2 = lambda a: a.reshape(1, -1).astype(F32)

    outs = {k: [] for k in ("p_ckv", "p_kr", "p_k", "p_v", "s_ckv", "s_kr", "s_k", "s_v")}

    def with_meta(real, small, tail):
        meta = jnp.broadcast_to(small[:N_META].reshape((1, N_META) + tail), (nb, N_META) + tail)
        return jnp.concatenate([meta, real.reshape((nb, seq) + tail)], axis=1)

    for i in range(depth):
        j = i // 2
        if i % 2 == 0:
            cos_r, sin_r = _rope_tables(pos_r, ROPE_A, ROPE_A)
            cos_s, sin_s = _rope_tables(pos_s, ROPE_A, ROPE_A)
            w_in = jnp.pad(mla_w_in[j], ((0, 0), (0, LANES - ROPE_A))).astype(BF16)
            wq = mla_w_uq[j].reshape(Q_LORA, H_A, NOPE_A + ROPE_A)
            wuq = jnp.concatenate([wq[:, :, :NOPE_A].reshape(Q_LORA, H_A * NOPE_A),
                                   wq[:, :, NOPE_A:].reshape(Q_LORA, H_A * ROPE_A)], axis=1).astype(BF16)
            wuk = jnp.swapaxes(mla_w_uk[j], 1, 2).astype(BF16)
            wuv = mla_w_uv[j].astype(BF16)
            gq, gkv = row2(mla_g_q[j]), row2(mla_g_kv[j])

            cq_r, ckv_r, kr_r, kb_r = _mla_in(hr, w_in, gq, gkv, cos_r, sin_r)
            cq_s, ckv_s, kr_s, kb_s = _mla_in(hs, w_in, gq, gkv, cos_s, sin_s)
            q_r = _mla_q(cq_r, wuq, wuk, cos_r, sin_r)
            q_s = _mla_q(cq_s, wuq, wuk, cos_s, sin_s)

            o_r = _mla_attn(q_r, kb_r, kb_s[:N_META], wuv, nb, seq)
            o_m = _mla_attn(q_s[:, :N_META], kb_s[:N_META], None, wuv, 1, N_META)
            olat = _mla_decode(pt_flat, jnp.swapaxes(q_s[:, N_META:], 0, 1), kb_s[N_META:].reshape(ndec, 1, KA_PAD),
                               cache_mla_ckv, cache_mla_krope, j, n_pages)
            o_d = _mla_uv(jnp.swapaxes(olat, 0, 1), wuv)
            w_o = mla_w_o[j].astype(BF16)

            outs["p_ckv"].append(with_meta(ckv_r, ckv_s, (KV_LORA,)))
            outs["p_kr"].append(with_meta(kr_r, kr_s, (ROPE_A,)))
            outs["s_ckv"].append(ckv_s[N_META:].reshape(ndec, 1, KV_LORA))
            outs["s_kr"].append(kr_s[N_META:].reshape(ndec, 1, ROPE_A))
        else:
            lam_init = 0.8 - 0.6 * math.exp(-0.3 * i)
            cos_r, sin_r = _rope_tables(pos_r, ROT_B, DH_B)
            cos_s, sin_s = _rope_tables(pos_s, ROT_B, DH_B)
            w_in = diff_w_in[j].astype(BF16)
            lam_vecs = jnp.stack([diff_lam_q1[j], diff_lam_k1[j], diff_lam_q2[j], diff_lam_k2[j]]).astype(F32)
            g_sub = row2(diff_g_sub[j])

            q_r, k_r, v_r, kb_r, vb_r = _diff_in(hr, w_in, cos_r, sin_r)
            q_s, k_s, v_s, kb_s, vb_s = _diff_in(hs, w_in, cos_s, sin_s)

            o_r = _diff_attn(q_r, kb_r, vb_r, kb_s[:N_META], vb_s[:N_META], lam_vecs, g_sub, lam_init, nb, seq)
            o_m = _diff_attn(q_s[:N_META], kb_s[:N_META], vb_s[:N_META], None, None, lam_vecs, g_sub, lam_init,
                             1, N_META)
            qd = q_s[N_META:].reshape(ndec, KVH_B, GROUP_B, 2, DH_B)
            qd = jnp.swapaxes(qd, 2, 3)
            eye = jnp.eye(2, dtype=BF16)
            qd = (qd[:, :, :, :, None, :] * eye[None, None, :, None, :, None]).reshape(ndec, KVH_B, 2 * GROUP_B, hd)
            qd = jnp.pad(qd, ((0, 0), (0, 0), (0, 8 - 2 * GROUP_B), (0, 0)))
            n_pool = cache_diff_k.shape[1]
            ck = cache_diff_k.reshape(cache_diff_k.shape[0], n_pool, PAGE * KVH_B, hd)
            cv = cache_diff_v.reshape(cache_diff_v.shape[0], n_pool, PAGE * KVH_B, hd)
            o_d = _diff_decode(pt_flat, qd, kb_s[N_META:].reshape(ndec, 1, KVH_B * hd),
                               vb_s[N_META:].reshape(ndec, 1, KVH_B * hd), lam_vecs, g_sub, ck, cv, j, n_pages,
                               lam_init)
            o_d = o_d.reshape(ndec, H_B * hd).astype(BF16)
            w_o = diff_w_o[j].astype(BF16)

            outs["p_k"].append(with_meta(k_r, k_s, (KVH_B, hd)))
            outs["p_v"].append(with_meta(v_r, v_s, (KVH_B, hd)))
            outs["s_k"].append(k_s[N_META:].reshape(ndec, 1, KVH_B, hd))
            outs["s_v"].append(v_s[N_META:].reshape(ndec, 1, KVH_B, hd))

        o_s = jnp.concatenate([o_m, o_d], axis=0)
        g1, b1, g2, b2 = row2(ln1_g[i]), row2(ln1_b[i]), row2(ln2_g[i]), row2(ln2_b[i])
        hr = _proj_ln(o_r, w_o, hr, g1, b1, alpha)
        hs = _proj_ln(o_s, w_o, hs, g1, b1, alpha)
        wg = ffn_w_gu[i][:, :d_ff].astype(BF16)
        wu = ffn_w_gu[i][:, d_ff:].astype(BF16)
        wd = ffn_w_down[i].astype(BF16)
        hr = _ffn(hr, wg, wu, wd, g2, b2, alpha)
        hs = _ffn(hs, wg, wu, wd, g2, b2, alpha)

    y_prompt = hr.reshape(nb, seq, d)
    y_sample = hs[N_META:].reshape(ndec, 1, d)
    st = lambda k: jnp.stack(outs[k], axis=0)
    return (y_prompt, y_sample, st("p_ckv"), st("p_kr"), st("p_k"), st("p_v"),
            st("s_ckv"), st("s_kr"), st("s_k"), st("s_v"))
```

```python
import functools
import math

import jax
import jax.numpy as jnp
from jax import lax
from jax.experimental import pallas as pl
from jax.experimental.pallas import tpu as pltpu

F32 = jnp.float32
BF16 = jnp.bfloat16

N_META = 16
PAGE = 128
ROPE_THETA = 500000.0
LN_EPS = 1e-5
RMS_EPS = 1e-6
H_A, NOPE_A, ROPE_A, V_A = 16, 64, 32, 64
Q_LORA, KV_LORA = 512, 256
LOG2E = math.log2(math.e)
SCALE_A = (NOPE_A + ROPE_A) ** -0.5 * LOG2E
H_B, KVH_B, DH_B = 8, 4, 64
GROUP_B = H_B // KVH_B
ROT_B = DH_B // 4
SCALE_B = DH_B ** -0.5 * LOG2E
LANES = 128
KA_PAD = KV_LORA + LANES

TM = 512
ATT_T = 512
HEAD_UNROLL = 4
SMALL_PAD = 256
FF_CHUNK = 256
MLA_PAGES_PER_CHUNK = 16
DIFF_PAGES_PER_CHUNK = 8
DEC_NBUF = 3
MIB = 1024 * 1024


def _params(sem, vmem_mib):
    return pltpu.CompilerParams(dimension_semantics=sem, vmem_limit_bytes=vmem_mib * MIB)


def _full(shape):
    nd = len(shape)
    return pl.BlockSpec(shape, lambda *_: (0,) * nd)


def _dot(a, b):
    return jnp.dot(a, b, preferred_element_type=F32)


def _dot_nt(a, b):
    return lax.dot_general(a, b, (((1,), (1,)), ((), ())), preferred_element_type=F32)


def _rms(x, g):
    return x * lax.rsqrt(jnp.mean(x * x, axis=-1, keepdims=True) + RMS_EPS) * g


def _layernorm(x, g, b):
    mu = jnp.mean(x, axis=-1, keepdims=True)
    xc = x - mu
    var = jnp.mean(xc * xc, axis=-1, keepdims=True)
    return xc * lax.rsqrt(var + LN_EPS) * g + b


def _rot_half(x, half):
    lane = lax.broadcasted_iota(jnp.int32, x.shape, 1)
    first = (lane % (2 * half)) < half
    return jnp.where(first, -pltpu.roll(x, LANES - half, 1), pltpu.roll(x, half, 1))


def _rope_rows(x1, x2, cos_t, sin_t):
    return x1 * cos_t - x2 * sin_t, x2 * cos_t + x1 * sin_t


def _mla_latent(x_ref, w_ref, gq_ref, gkv_ref, cos_ref, sin_ref):
    y = _dot(x_ref[...].astype(BF16), w_ref[...])
    cqn = _rms(y[:, :Q_LORA], gq_ref[...]).astype(BF16)
    ckv = _rms(y[:, Q_LORA:Q_LORA + KV_LORA], gkv_ref[...])
    kr = y[:, Q_LORA + KV_LORA:]
    kr = kr * cos_ref[...] + _rot_half(kr, ROPE_A // 2) * sin_ref[...]
    return cqn, ckv, kr


def _mla_in_kernel(x_ref, w_ref, gq_ref, gkv_ref, cos_ref, sin_ref, cq_ref, ckv_ref, kr_ref, kb_ref):
    cqn, ckv, kr = _mla_latent(x_ref, w_ref, gq_ref, gkv_ref, cos_ref, sin_ref)
    cq_ref[...] = cqn
    ckv_ref[...] = ckv
    kr_ref[...] = kr[:, :ROPE_A]
    kb_ref[...] = jnp.concatenate([ckv, kr], axis=1).astype(BF16)


def _mla_in(x, w, gq, gkv, cos, sin):
    m, d = x.shape
    return pl.pallas_call(
        _mla_in_kernel,
        in_specs=[_full(a.shape) for a in (x, w, gq, gkv, cos, sin)],
        out_specs=[_full((m, Q_LORA)), _full((m, KV_LORA)), _full((m, ROPE_A)), _full((m, KA_PAD))],
        out_shape=[jax.ShapeDtypeStruct((m, Q_LORA), BF16), jax.ShapeDtypeStruct((m, KV_LORA), F32),
                   jax.ShapeDtypeStruct((m, ROPE_A), F32), jax.ShapeDtypeStruct((m, KA_PAD), BF16)],
        compiler_params=_params(None, 32),
        name="mla_in",
    )(x, w, gq, gkv, cos, sin)


def _mla_q_kernel(cq_ref, wuq_ref, wuk_ref, cos_ref, sin_ref, q_ref):
    y = _dot(cq_ref[...], wuq_ref[...])
    cos, sin = cos_ref[...], sin_ref[...]
    lane = lax.broadcasted_iota(jnp.int32, cos.shape, 1)
    heads_per_slab = LANES // ROPE_A
    for slab in range(H_A // heads_per_slab):
        base = H_A * NOPE_A + slab * LANES
        sl = y[:, base:base + LANES]
        ro = (sl * cos + _rot_half(sl, ROPE_A // 2) * sin) * SCALE_A
        for k in range(heads_per_slab):
            h = slab * heads_per_slab + k
            piece = ro if k == 0 else pltpu.roll(ro, LANES - ROPE_A * k, 1)
            piece = jnp.where(lane < ROPE_A, piece, 0.0)
            qlat = _dot(y[:, h * NOPE_A:(h + 1) * NOPE_A].astype(BF16), wuk_ref[h]) * SCALE_A
            q_ref[h] = jnp.concatenate([qlat, piece], axis=1).astype(BF16)


def _mla_q(cq, wuq, wuk, cos, sin):
    m = cq.shape[0]
    return pl.pallas_call(
        _mla_q_kernel,
        in_specs=[_full(a.shape) for a in (cq, wuq, wuk, cos, sin)],
        out_specs=_full((H_A, m, KA_PAD)),
        out_shape=jax.ShapeDtypeStruct((H_A, m, KA_PAD), BF16),
        compiler_params=_params(None, 32),
        name="mla_q",
    )(cq, wuq, wuk, cos, sin)


def _mla_proj_kernel(x_ref, w_ref, gq_ref, gkv_ref, cos_ref, sin_ref, cos_t_ref, sin_t_ref, wk_ref, wv_t_ref,
                     wq_t_ref, ckv_ref, kr_ref, k_ref, v_t_ref, q_t_ref):
    cqn, ckv, kr = _mla_latent(x_ref, w_ref, gq_ref, gkv_ref, cos_ref, sin_ref)
    ckv_ref[...] = ckv
    kr_ref[...] = kr[:, :ROPE_A]
    ckv_b = ckv.astype(BF16)
    tm = ckv.shape[0]
    k_rope = pltpu.roll(kr, NOPE_A, 1)
    k_nope = _dot(ckv_b, wk_ref[...])
    for h in range(H_A):
        k_ref[h] = (k_nope[:, h * LANES:(h + 1) * LANES] + k_rope).astype(BF16)
    v_t_ref[...] = _dot_nt(wv_t_ref[...], ckv_b).reshape(H_A, V_A, tm).astype(BF16)
    q_t = _dot_nt(wq_t_ref[...], cqn)
    cos_t, sin_t = cos_t_ref[...], sin_t_ref[...]
    half = ROPE_A // 2
    for h in range(H_A):
        b0 = h * LANES
        r0 = b0 + NOPE_A
        r1, r2 = _rope_rows(q_t[r0:r0 + half], q_t[r0 + half:r0 + 2 * half], cos_t, sin_t)
        piece = jnp.concatenate([q_t[b0:r0], r1, r2, q_t[r0 + 2 * half:b0 + LANES]], axis=0)
        q_t_ref[h] = (piece * SCALE_A).astype(BF16)


def _mla_proj(x, w, gq, gkv, cos, sin, cos_t, sin_t, wk, wv_t, wq_t):
    m, d = x.shape
    tm = min(TM, m)
    nt = cos.shape[0] // tm
    row = lambda i: (i, 0)
    tab = lambda i: (i % nt, 0)
    tab_t = lambda i: (0, i % nt)
    return pl.pallas_call(
        _mla_proj_kernel,
        grid=(m // tm,),
        in_specs=[pl.BlockSpec((tm, d), row), _full(w.shape), _full(gq.shape), _full(gkv.shape),
                  pl.BlockSpec((tm, LANES), tab), pl.BlockSpec((tm, LANES), tab),
                  pl.BlockSpec((ROPE_A // 2, tm), tab_t), pl.BlockSpec((ROPE_A // 2, tm), tab_t),
                  _full(wk.shape), _full(wv_t.shape), _full(wq_t.shape)],
        out_specs=[pl.BlockSpec((tm, KV_LORA), row), pl.BlockSpec((tm, ROPE_A), row),
                   pl.BlockSpec((H_A, tm, LANES), lambda i: (0, i, 0)),
                   pl.BlockSpec((H_A, V_A, tm), lambda i: (0, 0, i)),
                   pl.BlockSpec((H_A, LANES, tm), lambda i: (0, 0, i))],
        out_shape=[jax.ShapeDtypeStruct((m, KV_LORA), F32), jax.ShapeDtypeStruct((m, ROPE_A), F32),
                   jax.ShapeDtypeStruct((H_A, m, LANES), BF16), jax.ShapeDtypeStruct((H_A, V_A, m), BF16),
                   jax.ShapeDtypeStruct((H_A, LANES, m), BF16)],
        compiler_params=_params(("parallel",), 48),
        name="mla_proj",
    )(x, w, gq, gkv, cos, sin, cos_t, sin_t, wk, wv_t, wq_t)


def _flash_tile(s_t, v_t, m_sc, l_sc, acc_sc, i):
    m_prev = m_sc[i]
    m_new = jnp.maximum(m_prev, jnp.max(s_t, axis=0, keepdims=True))
    a = jnp.exp2(m_prev - m_new)
    p = jnp.exp2(s_t - m_new)
    l_sc[i] = a * l_sc[i] + jnp.sum(p, axis=0, keepdims=True)
    acc_sc[i] = a * acc_sc[i] + _dot(v_t, p.astype(BF16))
    m_sc[i] = m_new


def _flash_reset(m_sc, l_sc, acc_sc):
    m_sc[...] = jnp.full(m_sc.shape, -jnp.inf, F32)
    l_sc[...] = jnp.zeros(l_sc.shape, F32)
    acc_sc[...] = jnp.zeros(acc_sc.shape, F32)


def _flash_prefix(score, value_t, n_maps, m_sc, l_sc, acc_sc, s_sc):
    for i in range(n_maps):
        s_sc[0, i * N_META:(i + 1) * N_META] = score(i)
    s = s_sc[0, :n_maps * N_META].reshape(n_maps, N_META, s_sc.shape[2])
    m = jnp.max(s, axis=1, keepdims=True)
    p = jnp.exp2(s - m)
    m_sc[...] = m
    l_sc[...] = jnp.sum(p, axis=1, keepdims=True)
    p = p.astype(BF16)
    for i in range(n_maps):
        acc_sc[i] = _dot(value_t(i), p[i])


def _causal_visible(t):
    return lax.broadcasted_iota(jnp.int32, (t, t), 0) <= lax.broadcasted_iota(jnp.int32, (t, t), 1)


def _mla_attn_kernel(*refs, t, has_prefix):
    if has_prefix:
        q_t_ref, k_ref, v_t_ref, km_ref, vm_t_ref, o_ref, m_sc, l_sc, acc_sc, s_sc = refs
    else:
        q_t_ref, k_ref, v_t_ref, o_ref, m_sc, l_sc, acc_sc, s_sc = refs
    qi, ki = pl.program_id(1), pl.program_id(2)

    def all_heads(k_src, v_t_src, visible):
        n_keys = k_src.shape[1]

        def scores(h, slot):
            s_sc[slot, :n_keys] = _dot(k_src[h], q_t_ref[h])

        def update(h, slot):
            s_t = s_sc[slot, :n_keys]
            if visible is not None:
                s_t = jnp.where(visible, s_t, -jnp.inf)
            _flash_tile(s_t, v_t_src[h], m_sc, l_sc, acc_sc, h)

        scores(0, 0)

        def body(i, c):
            h = 2 * i
            scores(h + 1, 1)
            update(h, 0)
            scores(jnp.minimum(h + 2, H_A - 1), 0)
            update(h + 1, 1)
            return c

        lax.fori_loop(0, H_A // 2, body, 0)

    @pl.when(ki == 0)
    def _init():
        if has_prefix:
            _flash_prefix(lambda h: _dot(km_ref[h], q_t_ref[h]), lambda h: vm_t_ref[h], H_A,
                          m_sc, l_sc, acc_sc, s_sc)
        else:
            _flash_reset(m_sc, l_sc, acc_sc)

    @pl.when(ki < qi)
    def _below_diagonal():
        all_heads(k_ref, v_t_ref, None)

    @pl.when(ki == qi)
    def _diagonal():
        all_heads(k_ref, v_t_ref, _causal_visible(t))
        o_t = (acc_sc[...] / l_sc[...]).reshape(H_A * V_A, t)
        o_ref[...] = o_t.T.astype(BF16)


def _mla_attn(q_t, k, v_t, k_meta, v_meta_t, nb, s, t):
    n = s // t
    has_prefix = k_meta is not None
    kv = lambda b, qi, ki: b * n + jnp.minimum(ki, qi)
    in_specs = [pl.BlockSpec((H_A, LANES, t), lambda b, qi, ki: (0, 0, b * n + qi)),
                pl.BlockSpec((H_A, t, LANES), lambda b, qi, ki: (0, kv(b, qi, ki), 0)),
                pl.BlockSpec((H_A, V_A, t), lambda b, qi, ki: (0, 0, kv(b, qi, ki)))]
    args = [q_t, k, v_t]
    if has_prefix:
        in_specs += [_full(k_meta.shape), _full(v_meta_t.shape)]
        args += [k_meta, v_meta_t]
    return pl.pallas_call(
        functools.partial(_mla_attn_kernel, t=t, has_prefix=has_prefix),
        grid=(nb, n, n),
        in_specs=in_specs,
        out_specs=pl.BlockSpec((t, H_A * V_A), lambda b, qi, ki: (b * n + qi, 0)),
        out_shape=jax.ShapeDtypeStruct((nb * s, H_A * V_A), BF16),
        scratch_shapes=[pltpu.VMEM((H_A, 1, t), F32), pltpu.VMEM((H_A, 1, t), F32),
                        pltpu.VMEM((H_A, V_A, t), F32), pltpu.VMEM((2, t, t), F32)],
        compiler_params=_params(("parallel", "parallel", "arbitrary"), 48),
        name="mla_attn",
    )(*args)


def _diff_lambda(lam_ref, lam_init):
    v = lam_ref[...]
    s1 = jnp.sum(v[0:1] * v[1:2], axis=-1, keepdims=True)
    s2 = jnp.sum(v[2:3] * v[3:4], axis=-1, keepdims=True)
    return jnp.exp(s1) - jnp.exp(s2) + lam_init


def _diff_attn_kernel(*refs, t, has_prefix, lam_init):
    if has_prefix:
        q_t_ref, k_ref, v_t_ref, km_ref, vm_t_ref, lam_ref, g_ref, o_ref, m_sc, l_sc, acc_sc, s_sc = refs
    else:
        q_t_ref, k_ref, v_t_ref, lam_ref, g_ref, o_ref, m_sc, l_sc, acc_sc, s_sc = refs
    qi, ki = pl.program_id(1), pl.program_id(2)
    hd = 2 * DH_B
    row = lax.broadcasted_iota(jnp.int32, (hd, t), 0)

    def all_maps(k_src, v_t_src, visible):
        n_keys = k_src.shape[0]

        def scores(i, slot):
            h, c = divmod(i, 2)
            n = h // GROUP_B
            q_t = q_t_ref[h]
            q_c = jnp.where((row < DH_B) == (c == 0), q_t, jnp.zeros_like(q_t))
            s_sc[slot, :n_keys] = _dot(k_src[:, n * hd:(n + 1) * hd], q_c)

        def update(i, slot):
            s_t = s_sc[slot, :n_keys]
            if visible is not None:
                s_t = jnp.where(visible, s_t, -jnp.inf)
            _flash_tile(s_t, v_t_src[i // (2 * GROUP_B)], m_sc, l_sc, acc_sc, i)

        n_maps = 2 * H_B
        scores(0, 0)
        for i in range(n_maps):
            if i + 1 < n_maps:
                scores(i + 1, (i + 1) % 2)
            update(i, i % 2)

    @pl.when(ki == 0)
    def _init():
        if has_prefix:
            def score(i):
                h, c = divmod(i, 2)
                n = h // GROUP_B
                q_t = q_t_ref[h]
                q_c = jnp.where((row < DH_B) == (c == 0), q_t, jnp.zeros_like(q_t))
                return _dot(km_ref[:, n * hd:(n + 1) * hd], q_c)

            _flash_prefix(score, lambda i: vm_t_ref[i // (2 * GROUP_B)], 2 * H_B, m_sc, l_sc, acc_sc, s_sc)
        else:
            _flash_reset(m_sc, l_sc, acc_sc)

    @pl.when(ki < qi)
    def _below_diagonal():
        all_maps(k_ref, v_t_ref, None)

    @pl.when(ki == qi)
    def _diagonal():
        all_maps(k_ref, v_t_ref, _causal_visible(t))
        lam = _diff_lambda(lam_ref, lam_init)
        g = g_ref[...] * (1.0 - lam_init)
        outs = []
        for h in range(H_B):
            o = acc_sc[2 * h] / l_sc[2 * h] - lam * (acc_sc[2 * h + 1] / l_sc[2 * h + 1])
            o = o * lax.rsqrt(jnp.mean(o * o, axis=0, keepdims=True) + RMS_EPS)
            outs.append(o * g)
        o_ref[...] = jnp.concatenate(outs, axis=0).T.astype(BF16)


def _diff_attn(q_t, k, v_t, k_meta, v_meta_t, lam_vecs, g_col, lam_init, nb, s, t):
    n = s // t
    has_prefix = k_meta is not None
    hd = 2 * DH_B
    kv = lambda b, qi, ki: b * n + jnp.minimum(ki, qi)
    in_specs = [pl.BlockSpec((H_B, hd, t), lambda b, qi, ki: (0, 0, b * n + qi)),
                pl.BlockSpec((t, KVH_B * hd), lambda b, qi, ki: (kv(b, qi, ki), 0)),
                pl.BlockSpec((KVH_B, hd, t), lambda b, qi, ki: (0, 0, kv(b, qi, ki)))]
    args = [q_t, k, v_t]
    if has_prefix:
        in_specs += [_full(k_meta.shape), _full(v_meta_t.shape)]
        args += [k_meta, v_meta_t]
    in_specs += [_full(lam_vecs.shape), _full(g_col.shape)]
    args += [lam_vecs, g_col]
    nvh = 2 * H_B
    return pl.pallas_call(
        functools.partial(_diff_attn_kernel, t=t, has_prefix=has_prefix, lam_init=lam_init),
        grid=(nb, n, n),
        in_specs=in_specs,
        out_specs=pl.BlockSpec((t, H_B * hd), lambda b, qi, ki: (b * n + qi, 0)),
        out_shape=jax.ShapeDtypeStruct((nb * s, H_B * hd), BF16),
        scratch_shapes=[pltpu.VMEM((nvh, 1, t), F32), pltpu.VMEM((nvh, 1, t), F32),
                        pltpu.VMEM((nvh, hd, t), F32), pltpu.VMEM((2, t, t), F32)],
        compiler_params=_params(("parallel", "parallel", "arbitrary"), 48),
        name="diff_attn",
    )(*args)


def _chunk_pipeline(n_chunks_total, start_chunk, wait_chunk, b, per_batch, consume, carry):
    g0 = b * per_batch

    @pl.when(b == 0)
    def _prime():
        for d in range(DEC_NBUF - 1):
            if d < n_chunks_total:
                start_chunk(d, d % DEC_NBUF)

    def body(c, carry):
        g = g0 + c
        slot = g % DEC_NBUF
        wait_chunk(slot)
        nxt = g + DEC_NBUF - 1

        @pl.when(nxt < n_chunks_total)
        def _():
            start_chunk(nxt, nxt % DEC_NBUF)

        return consume(slot, carry)

    return lax.fori_loop(0, per_batch, body, carry)


def _mla_decode_kernel(pt_ref, q_ref, knew_ref, ckv_hbm, kr_hbm, o_ref, ckv_buf, kr_buf, sem_c, sem_r,
                       *, layer, n_pages, cpp, n_batch):
    b = pl.program_id(0)
    per_batch = n_pages // cpp
    total = n_batch * per_batch

    def copies(g, slot, j):
        page = pt_ref[g * cpp + j]
        return (pltpu.make_async_copy(ckv_hbm.at[layer, page], ckv_buf.at[slot, j], sem_c.at[slot]),
                pltpu.make_async_copy(kr_hbm.at[layer, page], kr_buf.at[slot, :, pl.ds(j * PAGE, PAGE)],
                                      sem_r.at[slot]))

    def start_chunk(g, slot):
        for j in range(cpp):
            for cp in copies(g, slot, j):
                cp.start()

    def wait_chunk(slot):
        for j in range(cpp):
            for cp in copies(0, slot, j):
                cp.wait()

    q = q_ref[0]
    qlat, qr = q[:, :KV_LORA], q[:, KV_LORA:KV_LORA + ROPE_A]
    knew = knew_ref[0].astype(F32)
    m0 = jnp.sum(q.astype(F32) * knew, axis=-1, keepdims=True)
    l0 = jnp.ones_like(m0)
    acc0 = jnp.broadcast_to(knew[:, :KV_LORA], (H_A, KV_LORA))

    def consume(slot, carry):
        m_prev, l_prev, acc = carry
        kc = ckv_buf[slot].reshape(cpp * PAGE, KV_LORA).astype(BF16)
        kr_t = kr_buf[slot].astype(BF16)
        s = _dot_nt(qlat, kc) + _dot(qr, kr_t)
        m_new = jnp.maximum(m_prev, jnp.max(s, axis=-1, keepdims=True))
        a = jnp.exp2(m_prev - m_new)
        p = jnp.exp2(s - m_new)
        l_new = a * l_prev + jnp.sum(p, axis=-1, keepdims=True)
        return m_new, l_new, a * acc + _dot(p.astype(BF16), kc)

    _, l, acc = _chunk_pipeline(total, start_chunk, wait_chunk, b, per_batch, consume, (m0, l0, acc0))
    o_ref[0] = acc / l


def _mla_decode(pt_flat, q, knew, cache_ckv, cache_kr_t, layer, n_pages):
    nb = q.shape[0]
    cpp = min(MLA_PAGES_PER_CHUNK, n_pages)
    grid_spec = pltpu.PrefetchScalarGridSpec(
        num_scalar_prefetch=1,
        grid=(nb,),
        in_specs=[pl.BlockSpec((1, H_A, KA_PAD), lambda b, pt: (b, 0, 0)),
                  pl.BlockSpec((1, 1, KA_PAD), lambda b, pt: (b, 0, 0)),
                  pl.BlockSpec(memory_space=pl.ANY), pl.BlockSpec(memory_space=pl.ANY)],
        out_specs=pl.BlockSpec((1, H_A, KV_LORA), lambda b, pt: (b, 0, 0)),
        scratch_shapes=[pltpu.VMEM((DEC_NBUF, cpp, PAGE, KV_LORA), F32),
                        pltpu.VMEM((DEC_NBUF, ROPE_A, cpp * PAGE), F32),
                        pltpu.SemaphoreType.DMA((DEC_NBUF,)), pltpu.SemaphoreType.DMA((DEC_NBUF,))],
    )
    return pl.pallas_call(
        functools.partial(_mla_decode_kernel, layer=layer, n_pages=n_pages, cpp=cpp, n_batch=nb),
        grid_spec=grid_spec,
        out_shape=jax.ShapeDtypeStruct((nb, H_A, KV_LORA), F32),
        compiler_params=_params(("arbitrary",), 32),
        name="mla_decode",
    )(pt_flat, q, knew, cache_ckv, cache_kr_t)


def _mla_uv_kernel(o_ref, wuv_ref, out_ref):
    outs = [_dot(o_ref[h].astype(BF16), wuv_ref[h]) for h in range(H_A)]
    out_ref[...] = jnp.concatenate(outs, axis=1).astype(BF16)


def _mla_uv(olat, wuv):
    m = olat.shape[1]
    return pl.pallas_call(
        _mla_uv_kernel,
        in_specs=[_full(olat.shape), _full(wuv.shape)],
        out_specs=_full((m, H_A * V_A)),
        out_shape=jax.ShapeDtypeStruct((m, H_A * V_A), BF16),
        name="mla_uv",
    )(olat, wuv)


def _diff_decode_kernel(pt_ref, q_ref, knew_ref, vnew_ref, lam_ref, g_ref, k_hbm, v_hbm, o_ref,
                        k_buf, v_buf, sem_k, sem_v, *, layer, n_pages, cpp, n_batch, lam_init):
    b = pl.program_id(0)
    per_batch = n_pages // cpp
    total = n_batch * per_batch
    hd = 2 * DH_B
    prow = PAGE * KVH_B
    rows = 2 * GROUP_B

    def copies(g, slot, j):
        page = pt_ref[g * cpp + j]
        dst = pl.ds(j * prow, prow)
        return (pltpu.make_async_copy(k_hbm.at[layer, page], k_buf.at[slot, dst], sem_k.at[slot]),
                pltpu.make_async_copy(v_hbm.at[layer, page], v_buf.at[slot, dst], sem_v.at[slot]))

    def start_chunk(g, slot):
        for j in range(cpp):
            for cp in copies(g, slot, j):
                cp.start()

    def wait_chunk(slot):
        for j in range(cpp):
            for cp in copies(0, slot, j):
                cp.wait()

    qs = [q_ref[0, n] for n in range(KVH_B)]
    knew = knew_ref[0].astype(F32)
    vnew = vnew_ref[0].astype(F32)
    carry = []
    for n in range(KVH_B):
        m0 = jnp.sum(qs[n].astype(F32) * knew[:, n * hd:(n + 1) * hd], axis=-1, keepdims=True)
        carry += [m0, jnp.ones_like(m0), jnp.broadcast_to(vnew[:, n * hd:(n + 1) * hd], (8, hd))]

    def consume(slot, carry):
        out = []
        for n in range(KVH_B):
            m_prev, l_prev, acc = carry[3 * n:3 * n + 3]
            sel = pl.ds(n, cpp * PAGE, stride=KVH_B)
            kn = k_buf[slot, sel, :].astype(BF16)
            vn = v_buf[slot, sel, :].astype(BF16)
            s = _dot_nt(qs[n], kn)
            m_new = jnp.maximum(m_prev, jnp.max(s, axis=-1, keepdims=True))
            a = jnp.exp2(m_prev - m_new)
            p = jnp.exp2(s - m_new)
            out += [m_new, a * l_prev + jnp.sum(p, axis=-1, keepdims=True), a * acc + _dot(p.astype(BF16), vn)]
        return tuple(out)

    carry = _chunk_pipeline(total, start_chunk, wait_chunk, b, per_batch, consume, tuple(carry))
    lam = _diff_lambda(lam_ref, lam_init)
    heads = []
    for n in range(KVH_B):
        o = carry[3 * n + 2] / carry[3 * n + 1]
        heads.append(o[0:GROUP_B] - lam * o[GROUP_B:rows])
    o = jnp.concatenate(heads, axis=0)
    o_ref[0] = _rms(o, g_ref[...]) * (1.0 - lam_init)


def _diff_decode(pt_flat, q, knew, vnew, lam_vecs, g_sub, cache_k, cache_v, layer, n_pages, lam_init):
    nb = q.shape[0]
    cpp = min(DIFF_PAGES_PER_CHUNK, n_pages)
    hd = 2 * DH_B
    kw = KVH_B * hd
    grid_spec = pltpu.PrefetchScalarGridSpec(
        num_scalar_prefetch=1,
        grid=(nb,),
        in_specs=[pl.BlockSpec((1, KVH_B, 8, hd), lambda b, pt: (b, 0, 0, 0)),
                  pl.BlockSpec((1, 1, kw), lambda b, pt: (b, 0, 0)),
                  pl.BlockSpec((1, 1, kw), lambda b, pt: (b, 0, 0)),
                  pl.BlockSpec(lam_vecs.shape, lambda b, pt: (0, 0)),
                  pl.BlockSpec(g_sub.shape, lambda b, pt: (0, 0)),
                  pl.BlockSpec(memory_space=pl.ANY), pl.BlockSpec(memory_space=pl.ANY)],
        out_specs=pl.BlockSpec((1, H_B, hd), lambda b, pt: (b, 0, 0)),
        scratch_shapes=[pltpu.VMEM((DEC_NBUF, cpp * PAGE * KVH_B, hd), F32),
                        pltpu.VMEM((DEC_NBUF, cpp * PAGE * KVH_B, hd), F32),
                        pltpu.SemaphoreType.DMA((DEC_NBUF,)), pltpu.SemaphoreType.DMA((DEC_NBUF,))],
    )
    return pl.pallas_call(
        functools.partial(_diff_decode_kernel, layer=layer, n_pages=n_pages, cpp=cpp, n_batch=nb,
                          lam_init=lam_init),
        grid_spec=grid_spec,
        out_shape=jax.ShapeDtypeStruct((nb, H_B, hd), F32),
        compiler_params=_params(("arbitrary",), 40),
        name="diff_decode",
    )(pt_flat, q, knew, vnew, lam_vecs, g_sub, cache_k, cache_v)


def _diff_in_kernel(x_ref, wkv_ref, wq_t_ref, wv_t_ref, cos_ref, sin_ref, cos_t_ref, sin_t_ref,
                    k_ref, v_ref, kb_ref, q_t_ref, v_t_ref):
    xb = x_ref[...].astype(BF16)
    tm = xb.shape[0]
    nk = KVH_B * 2 * DH_B
    hd = 2 * DH_B
    y = _dot(xb, wkv_ref[...])
    cos, sin = cos_ref[...], sin_ref[...]
    parts = []
    for c0 in range(0, nk, LANES):
        sl = y[:, c0:c0 + LANES]
        parts.append(sl * cos + _rot_half(sl, ROT_B // 2) * sin)
    k = jnp.concatenate(parts, axis=1)
    k_ref[...] = k
    kb_ref[...] = k.astype(BF16)
    v_ref[...] = y[:, nk:]
    v_t_ref[...] = _dot_nt(wv_t_ref[...], xb).reshape(KVH_B, hd, tm).astype(BF16)
    q_t = _dot_nt(wq_t_ref[...], xb)
    cos_t, sin_t = cos_t_ref[...], sin_t_ref[...]
    half = ROT_B // 2
    pieces = []
    for b0 in range(0, H_B * hd, DH_B):
        r1, r2 = _rope_rows(q_t[b0:b0 + half], q_t[b0 + half:b0 + 2 * half], cos_t, sin_t)
        pieces += [r1, r2, q_t[b0 + 2 * half:b0 + DH_B]]
    q_t_ref[...] = (jnp.concatenate(pieces, axis=0) * SCALE_B).reshape(H_B, hd, tm).astype(BF16)


def _diff_in(x, wkv, wq_t, wv_t, cos, sin, cos_t, sin_t):
    m, d = x.shape
    tm = min(TM, m)
    nt = cos.shape[0] // tm
    nk = KVH_B * 2 * DH_B
    hd = 2 * DH_B
    row = lambda i: (i, 0)
    tab = lambda i: (i % nt, 0)
    tab_t = lambda i: (0, i % nt)
    return pl.pallas_call(
        _diff_in_kernel,
        grid=(m // tm,),
        in_specs=[pl.BlockSpec((tm, d), row), _full(wkv.shape), _full(wq_t.shape), _full(wv_t.shape),
                  pl.BlockSpec((tm, LANES), tab), pl.BlockSpec((tm, LANES), tab),
                  pl.BlockSpec((ROT_B // 2, tm), tab_t), pl.BlockSpec((ROT_B // 2, tm), tab_t)],
        out_specs=[pl.BlockSpec((tm, nk), row), pl.BlockSpec((tm, nk), row), pl.BlockSpec((tm, nk), row),
                   pl.BlockSpec((H_B, hd, tm), lambda i: (0, 0, i)),
                   pl.BlockSpec((KVH_B, hd, tm), lambda i: (0, 0, i))],
        out_shape=[jax.ShapeDtypeStruct((m, nk), F32), jax.ShapeDtypeStruct((m, nk), F32),
                   jax.ShapeDtypeStruct((m, nk), BF16), jax.ShapeDtypeStruct((H_B, hd, m), BF16),
                   jax.ShapeDtypeStruct((KVH_B, hd, m), BF16)],
        compiler_params=_params(("parallel",), 48),
        name="diff_in",
    )(x, wkv, wq_t, wv_t, cos, sin, cos_t, sin_t)


def _proj_ln_kernel(o_ref, w_ref, x_ref, g_ref, b_ref, out_ref, *, alpha):
    mix = _dot(o_ref[...], w_ref[...])
    out_ref[...] = _layernorm(alpha * x_ref[...] + mix, g_ref[...], b_ref[...])


def _proj_ln(o, w, x, g, b, alpha):
    m, d = x.shape
    tm = min(TM, m)
    row = lambda i: (i, 0)
    return pl.pallas_call(
        functools.partial(_proj_ln_kernel, alpha=alpha),
        grid=(m // tm,),
        in_specs=[pl.BlockSpec((tm, o.shape[1]), row), _full(w.shape), pl.BlockSpec((tm, d), row),
                  _full(g.shape), _full(b.shape)],
        out_specs=pl.BlockSpec((tm, d), row),
        out_shape=jax.ShapeDtypeStruct((m, d), F32),
        compiler_params=_params(("parallel",), 32),
        name="proj_ln",
    )(o, w, x, g, b)


def _ffn_kernel(x_ref, wg_ref, wu_ref, wd_ref, g_ref, b_ref, out_ref, acc_ref, *, alpha, d_ff):
    x = x_ref[...]
    xb = x.astype(BF16)
    for i, c0 in enumerate(range(0, d_ff, FF_CHUNK)):
        gate = _dot(xb, wg_ref[:, c0:c0 + FF_CHUNK])
        up = _dot(xb, wu_ref[:, c0:c0 + FF_CHUNK])
        act = (gate * jax.nn.sigmoid(gate) * up).astype(BF16)
        part = _dot(act, wd_ref[c0:c0 + FF_CHUNK, :])
        if i == 0:
            acc_ref[...] = part
        else:
            acc_ref[...] += part
    out_ref[...] = _layernorm(alpha * x + acc_ref[...], g_ref[...], b_ref[...])


def _ffn(x, wg, wu, wd, g, b, alpha):
    m, d = x.shape
    d_ff = wg.shape[1]
    tm = min(TM, m)
    row = lambda i: (i, 0)
    resident = lambda shape: pl.BlockSpec(shape, lambda i: (0, 0), pipeline_mode=pl.Buffered(1))
    return pl.pallas_call(
        functools.partial(_ffn_kernel, alpha=alpha, d_ff=d_ff),
        grid=(m // tm,),
        in_specs=[pl.BlockSpec((tm, d), row), resident(wg.shape), resident(wu.shape), resident(wd.shape),
                  _full(g.shape), _full(b.shape)],
        out_specs=pl.BlockSpec((tm, d), row),
        out_shape=jax.ShapeDtypeStruct((m, d), F32),
        scratch_shapes=[pltpu.VMEM((tm, d), F32)],
        compiler_params=_params(("parallel",), 48),
        name="ffn",
    )(x, wg, wu, wd, g, b)


def _rope_angles(pos, d):
    inv = jnp.power(ROPE_THETA, -jnp.arange(0, d, 2, dtype=F32) / d)
    ang = pos.astype(F32)[:, None] * inv[None, :]
    return jnp.cos(ang), jnp.sin(ang)


def _rope_tables(pos, d, period):
    cos, sin = _rope_angles(pos, d)
    n = pos.shape[0]
    cos_row = jnp.concatenate([cos, cos, jnp.ones((n, period - d), F32)], axis=1)
    sin_row = jnp.concatenate([sin, sin, jnp.zeros((n, period - d), F32)], axis=1)
    reps = LANES // period
    return jnp.tile(cos_row, (1, reps)), jnp.tile(sin_row, (1, reps)), cos.T, sin.T


def kernel(x_prompt, x_sample, cache_mla_ckv, cache_mla_krope, cache_diff_k, cache_diff_v, page_table, meta_tokens, mla_w_in, mla_g_q, mla_g_kv, mla_w_uq, mla_w_uk, mla_w_uv, mla_w_o, diff_w_in, diff_lam_q1, diff_lam_k1, diff_lam_q2, diff_lam_k2, diff_g_sub, diff_w_o, ffn_w_gu, ffn_w_down, ln1_g, ln1_b, ln2_g, ln2_b):
    nb, seq, d = x_prompt.shape
    ndec, dec_seq, _ = x_sample.shape
    assert dec_seq == 1 and meta_tokens.shape[0] == N_META
    n_small = N_META + ndec
    assert n_small <= SMALL_PAD and seq % min(ATT_T, seq) == 0
    n_pages = page_table.shape[1]
    past = n_pages * PAGE
    depth = ln1_g.shape[0]
    alpha = (2 * depth) ** 0.25
    d_ff = ffn_w_down.shape[1]
    hd = 2 * DH_B
    t_att = min(ATT_T, seq)

    hr = x_prompt.reshape(nb * seq, d)
    hs = jnp.concatenate([meta_tokens.astype(F32), x_sample.reshape(ndec, d)], axis=0)
    pos_r = N_META + jnp.arange(seq, dtype=jnp.int32)
    pos_s = jnp.concatenate([jnp.arange(N_META, dtype=jnp.int32), jnp.full((SMALL_PAD - N_META,), past, jnp.int32)])
    pt_flat = page_table.reshape(-1).astype(jnp.int32)
    row2 = lambda a: a.reshape(1, -1).astype(F32)
    pad_small = lambda a: jnp.pad(a, ((0, SMALL_PAD - n_small), (0, 0)))
    dec = slice(N_META, n_small)

    outs = {k: [] for k in ("p_ckv", "p_kr", "p_k", "p_v", "s_ckv", "s_kr", "s_k", "s_v")}

    def with_meta(real, small, tail):
        meta = jnp.broadcast_to(small[:N_META].reshape((1, N_META) + tail), (nb, N_META) + tail)
        return jnp.concatenate([meta, real.reshape((nb, seq) + tail)], axis=1)

    for i in range(depth):
        j = i // 2
        if i % 2 == 0:
            tr = _rope_tables(pos_r, ROPE_A, ROPE_A)
            ts = _rope_tables(pos_s, ROPE_A, ROPE_A)
            w_in = jnp.pad(mla_w_in[j], ((0, 0), (0, LANES - ROPE_A))).astype(BF16)
            wq = mla_w_uq[j].reshape(Q_LORA, H_A, NOPE_A + ROPE_A)
            wuq = jnp.concatenate([wq[:, :, :NOPE_A].reshape(Q_LORA, H_A * NOPE_A),
                                   wq[:, :, NOPE_A:].reshape(Q_LORA, H_A * ROPE_A)], axis=1).astype(BF16)
            wuk_t = jnp.swapaxes(mla_w_uk[j], 1, 2).astype(BF16)
            wuv = mla_w_uv[j].astype(BF16)
            wk = jnp.pad(mla_w_uk[j], ((0, 0), (0, 0), (0, LANES - NOPE_A)))
            wk = jnp.swapaxes(wk, 0, 1).reshape(KV_LORA, H_A * LANES).astype(BF16)
            wv_t = jnp.swapaxes(mla_w_uv[j], 1, 2).reshape(H_A * V_A, KV_LORA).astype(BF16)
            wq_t = jnp.pad(wq, ((0, 0), (0, 0), (0, LANES - NOPE_A - ROPE_A)))
            wq_t = jnp.transpose(wq_t, (1, 2, 0)).reshape(H_A * LANES, Q_LORA).astype(BF16)
            gq, gkv = row2(mla_g_q[j]), row2(mla_g_kv[j])

            ckv_r, kr_r, k_r, vt_r, qt_r = _mla_proj(hr, w_in, gq, gkv, *tr, wk, wv_t, wq_t)
            _, _, k_s, vt_s, qt_s = _mla_proj(pad_small(hs), w_in, gq, gkv, *ts, wk, wv_t, wq_t)
            cq_s, ckv_s, kr_s, kb_s = _mla_in(hs, w_in, gq, gkv, ts[0][:n_small], ts[1][:n_small])
            q_s = _mla_q(cq_s, wuq, wuk_t, ts[0][:n_small], ts[1][:n_small])

            k_m, vt_m, qt_m = k_s[:, :LANES], vt_s[:, :, :LANES], qt_s[:, :, :LANES]
            o_r = _mla_attn(qt_r, k_r, vt_r, k_s[:, :N_META], vt_s[:, :, :N_META], nb, seq, t_att)
            o_m = _mla_attn(qt_m, k_m, vt_m, None, None, 1, LANES, LANES)[:N_META]
            olat = _mla_decode(pt_flat, jnp.swapaxes(q_s[:, dec], 0, 1), kb_s[dec].reshape(ndec, 1, KA_PAD),
                               cache_mla_ckv, jnp.swapaxes(cache_mla_krope, 2, 3), j, n_pages)
            o_d = _mla_uv(jnp.swapaxes(olat, 0, 1), wuv)
            w_o = mla_w_o[j].astype(BF16)

            outs["p_ckv"].append(with_meta(ckv_r, ckv_s, (KV_LORA,)))
            outs["p_kr"].append(with_meta(kr_r, kr_s, (ROPE_A,)))
            outs["s_ckv"].append(ckv_s[dec].reshape(ndec, 1, KV_LORA))
            outs["s_kr"].append(kr_s[dec].reshape(ndec, 1, ROPE_A))
        else:
            lam_init = 0.8 - 0.6 * math.exp(-0.3 * i)
            tr = _rope_tables(pos_r, ROT_B, DH_B)
            ts = _rope_tables(pos_s, ROT_B, DH_B)
            nq, nk = H_B * hd, KVH_B * hd
            w = diff_w_in[j]
            wkv = w[:, nq:].astype(BF16)
            wq_t = w[:, :nq].T.astype(BF16)
            wv_t = w[:, nq + nk:].T.astype(BF16)
            lam_vecs = jnp.stack([diff_lam_q1[j], diff_lam_k1[j], diff_lam_q2[j], diff_lam_k2[j]]).astype(F32)
            g_sub = row2(diff_g_sub[j])

            k_r, v_r, kb_r, qt_r, vt_r = _diff_in(hr, wkv, wq_t, wv_t, *tr)
            k_s, v_s, kb_s, qt_s, vt_s = _diff_in(pad_small(hs), wkv, wq_t, wv_t, *ts)

            k_m, vt_m, qt_m = kb_s[:LANES], vt_s[:, :, :LANES], qt_s[:, :, :LANES]
            o_r = _diff_attn(qt_r, kb_r, vt_r, kb_s[:N_META], vt_s[:, :, :N_META], lam_vecs, g_sub.T, lam_init,
                             nb, seq, t_att)
            o_m = _diff_attn(qt_m, k_m, vt_m, None, None, lam_vecs, g_sub.T, lam_init, 1, LANES, LANES)[:N_META]
            qd = jnp.transpose(qt_s[:, :, dec], (2, 0, 1)).reshape(ndec, KVH_B, GROUP_B, 2, DH_B)
            qd = jnp.swapaxes(qd, 2, 3)
            eye = jnp.eye(2, dtype=BF16)
            qd = (qd[:, :, :, :, None, :] * eye[None, None, :, None, :, None]).reshape(ndec, KVH_B, 2 * GROUP_B, hd)
            qd = jnp.pad(qd, ((0, 0), (0, 0), (0, 8 - 2 * GROUP_B), (0, 0)))
            n_pool = cache_diff_k.shape[1]
            ck = cache_diff_k.reshape(cache_diff_k.shape[0], n_pool, PAGE * KVH_B, hd)
            cv = cache_diff_v.reshape(cache_diff_v.shape[0], n_pool, PAGE * KVH_B, hd)
            o_d = _diff_decode(pt_flat, qd, kb_s[dec].reshape(ndec, 1, nk),
                               v_s[dec].astype(BF16).reshape(ndec, 1, nk), lam_vecs, g_sub, ck, cv, j, n_pages,
                               lam_init)
            o_d = o_d.reshape(ndec, H_B * hd).astype(BF16)
            w_o = diff_w_o[j].astype(BF16)

            outs["p_k"].append(with_meta(k_r, k_s, (KVH_B, hd)))
            outs["p_v"].append(with_meta(v_r, v_s, (KVH_B, hd)))
            outs["s_k"].append(k_s[dec].reshape(ndec, 1, KVH_B, hd))
            outs["s_v"].append(v_s[dec].reshape(ndec, 1, KVH_B, hd))

        o_s = jnp.concatenate([o_m, o_d], axis=0)
        g1, b1, g2, b2 = row2(ln1_g[i]), row2(ln1_b[i]), row2(ln2_g[i]), row2(ln2_b[i])
        hr = _proj_ln(o_r, w_o, hr, g1, b1, alpha)
        hs = _proj_ln(o_s, w_o, hs, g1, b1, alpha)
        wg = ffn_w_gu[i][:, :d_ff].astype(BF16)
        wu = ffn_w_gu[i][:, d_ff:].astype(BF16)
        wd = ffn_w_down[i].astype(BF16)
        hr = _ffn(hr, wg, wu, wd, g2, b2, alpha)
        hs = _ffn(hs, wg, wu, wd, g2, b2, alpha)

    y_prompt = hr.reshape(nb, seq, d)
    y_sample = hs[dec].reshape(ndec, 1, d)
    st = lambda k: jnp.stack(outs[k], axis=0)
    return (y_prompt, y_sample, st("p_ckv"), st("p_kr"), st("p_k"), st("p_v"),
            st("s_ckv"), st("s_kr"), st("s_k"), st("s_v"))
```

```python
import functools
import math

import jax
import jax.numpy as jnp
from jax import lax
from jax.experimental import pallas as pl
from jax.experimental.pallas import tpu as pltpu

F32 = jnp.float32
BF16 = jnp.bfloat16

N_META = 16
PAGE = 128
ROPE_THETA = 500000.0
LN_EPS = 1e-5
RMS_EPS = 1e-6
H_A, NOPE_A, ROPE_A, V_A = 16, 64, 32, 64
Q_LORA, KV_LORA = 512, 256
LOG2E = math.log2(math.e)
SCALE_A = (NOPE_A + ROPE_A) ** -0.5 * LOG2E
H_B, KVH_B, DH_B = 8, 4, 64
GROUP_B = H_B // KVH_B
ROT_B = DH_B // 4
SCALE_B = DH_B ** -0.5 * LOG2E
LANES = 128
KA_PAD = KV_LORA + LANES

TM = 512
ATT_T = 512
SMALL_PAD = 256
FF_CHUNK = 256
MLA_PAGES_PER_CHUNK = 16
DIFF_PAGES_PER_CHUNK = 8
DEC_NBUF = 3
MIB = 1024 * 1024


def _params(sem, vmem_mib):
    return pltpu.CompilerParams(dimension_semantics=sem, vmem_limit_bytes=vmem_mib * MIB)


def _full(shape):
    nd = len(shape)
    return pl.BlockSpec(shape, lambda *_: (0,) * nd)


def _dot(a, b):
    return jnp.dot(a, b, preferred_element_type=F32)


def _dot_nt(a, b):
    return lax.dot_general(a, b, (((1,), (1,)), ((), ())), preferred_element_type=F32)


def _rms(x, g):
    return x * lax.rsqrt(jnp.mean(x * x, axis=-1, keepdims=True) + RMS_EPS) * g


def _layernorm(x, g, b):
    mu = jnp.mean(x, axis=-1, keepdims=True)
    xc = x - mu
    var = jnp.mean(xc * xc, axis=-1, keepdims=True)
    return xc * lax.rsqrt(var + LN_EPS) * g + b


def _rot_half(x, half):
    lane = lax.broadcasted_iota(jnp.int32, x.shape, 1)
    first = (lane % (2 * half)) < half
    return jnp.where(first, -pltpu.roll(x, LANES - half, 1), pltpu.roll(x, half, 1))


def _rope_rows(x1, x2, cos_t, sin_t):
    return x1 * cos_t - x2 * sin_t, x2 * cos_t + x1 * sin_t


def _mla_latent(x_ref, w_ref, gq_ref, gkv_ref, cos_ref, sin_ref):
    y = _dot(x_ref[...].astype(BF16), w_ref[...])
    cqn = _rms(y[:, :Q_LORA], gq_ref[...]).astype(BF16)
    ckv = _rms(y[:, Q_LORA:Q_LORA + KV_LORA], gkv_ref[...])
    kr = y[:, Q_LORA + KV_LORA:]
    kr = kr * cos_ref[...] + _rot_half(kr, ROPE_A // 2) * sin_ref[...]
    return cqn, ckv, kr


def _mla_in_kernel(x_ref, w_ref, gq_ref, gkv_ref, cos_ref, sin_ref, cq_ref, ckv_ref, kr_ref, kb_ref):
    cqn, ckv, kr = _mla_latent(x_ref, w_ref, gq_ref, gkv_ref, cos_ref, sin_ref)
    cq_ref[...] = cqn
    ckv_ref[...] = ckv
    kr_ref[...] = kr[:, :ROPE_A]
    kb_ref[...] = jnp.concatenate([ckv, kr], axis=1).astype(BF16)


def _mla_in(x, w, gq, gkv, cos, sin):
    m, d = x.shape
    return pl.pallas_call(
        _mla_in_kernel,
        in_specs=[_full(a.shape) for a in (x, w, gq, gkv, cos, sin)],
        out_specs=[_full((m, Q_LORA)), _full((m, KV_LORA)), _full((m, ROPE_A)), _full((m, KA_PAD))],
        out_shape=[jax.ShapeDtypeStruct((m, Q_LORA), BF16), jax.ShapeDtypeStruct((m, KV_LORA), F32),
                   jax.ShapeDtypeStruct((m, ROPE_A), F32), jax.ShapeDtypeStruct((m, KA_PAD), BF16)],
        compiler_params=_params(None, 32),
        name="mla_in",
    )(x, w, gq, gkv, cos, sin)


def _mla_q_kernel(cq_ref, wuq_ref, wuk_ref, cos_ref, sin_ref, q_ref):
    y = _dot(cq_ref[...], wuq_ref[...])
    cos, sin = cos_ref[...], sin_ref[...]
    lane = lax.broadcasted_iota(jnp.int32, cos.shape, 1)
    heads_per_slab = LANES // ROPE_A
    for slab in range(H_A // heads_per_slab):
        base = H_A * NOPE_A + slab * LANES
        sl = y[:, base:base + LANES]
        ro = (sl * cos + _rot_half(sl, ROPE_A // 2) * sin) * SCALE_A
        for k in range(heads_per_slab):
            h = slab * heads_per_slab + k
            piece = ro if k == 0 else pltpu.roll(ro, LANES - ROPE_A * k, 1)
            piece = jnp.where(lane < ROPE_A, piece, 0.0)
            qlat = _dot(y[:, h * NOPE_A:(h + 1) * NOPE_A].astype(BF16), wuk_ref[h]) * SCALE_A
            q_ref[h] = jnp.concatenate([qlat, piece], axis=1).astype(BF16)


def _mla_q(cq, wuq, wuk, cos, sin):
    m = cq.shape[0]
    return pl.pallas_call(
        _mla_q_kernel,
        in_specs=[_full(a.shape) for a in (cq, wuq, wuk, cos, sin)],
        out_specs=_full((H_A, m, KA_PAD)),
        out_shape=jax.ShapeDtypeStruct((H_A, m, KA_PAD), BF16),
        compiler_params=_params(None, 32),
        name="mla_q",
    )(cq, wuq, wuk, cos, sin)


def _store_rows(refs_and_rows, meta_refs):
    for ref, rows in refs_and_rows:
        tm, width = rows.shape
        split = width // ref.shape[-1]
        if meta_refs is None:
            ref[...] = rows
        elif split == 1:
            ref[0, pl.ds(pl.multiple_of(N_META + pl.program_id(1) * tm, 8), tm)] = rows
        else:
            base = pl.multiple_of((N_META + pl.program_id(1) * tm) * split, 8)
            for n in range(split):
                ref[0, pl.ds(base + n, tm, stride=split)] = rows[:, n * ref.shape[-1]:(n + 1) * ref.shape[-1]]


def _store_meta_rows(refs, meta_refs):
    if meta_refs is None:
        return

    @pl.when(pl.program_id(1) == 0)
    def _():
        for ref, meta in zip(refs, meta_refs):
            ref[0, :meta.shape[0]] = meta[...]


def _mla_proj_kernel(*refs, with_meta):
    (x_ref, w_ref, gq_ref, gkv_ref, cos_ref, sin_ref, cos_t_ref, sin_t_ref, wk_ref, wv_t_ref, wq_t_ref) = refs[:11]
    meta_refs = refs[11:13] if with_meta else None
    ckv_ref, kr_ref, k_ref, v_t_ref, q_t_ref = refs[-5:]
    cqn, ckv, kr = _mla_latent(x_ref, w_ref, gq_ref, gkv_ref, cos_ref, sin_ref)
    _store_rows([(ckv_ref, ckv), (kr_ref, kr[:, :ROPE_A])], meta_refs)
    ckv_b = ckv.astype(BF16)
    tm = ckv.shape[0]
    k_rope = pltpu.roll(kr, NOPE_A, 1)
    k_nope = _dot(ckv_b, wk_ref[...])
    for h in range(H_A):
        k_ref[h] = (k_nope[:, h * LANES:(h + 1) * LANES] + k_rope).astype(BF16)
    v_t_ref[...] = _dot_nt(wv_t_ref[...], ckv_b).reshape(H_A, V_A, tm).astype(BF16)
    q_t = _dot_nt(wq_t_ref[...], cqn)
    cos_t, sin_t = cos_t_ref[...], sin_t_ref[...]
    half = ROPE_A // 2
    for h in range(H_A):
        b0 = h * LANES
        r0 = b0 + NOPE_A
        r1, r2 = _rope_rows(q_t[r0:r0 + half], q_t[r0 + half:r0 + 2 * half], cos_t, sin_t)
        piece = jnp.concatenate([q_t[b0:r0], r1, r2, q_t[r0 + 2 * half:b0 + LANES]], axis=0)
        q_t_ref[h] = (piece * SCALE_A).astype(BF16)
    _store_meta_rows((ckv_ref, kr_ref), meta_refs)


def _row_specs(m, tm, nt, widths, metas, split=1):
    if metas is None:
        return ([pl.BlockSpec((tm, w), lambda b, j: (b * nt + j, 0)) for w in widths],
                [jax.ShapeDtypeStruct((m, w), F32) for w in widths])
    rows = (N_META + nt * tm) * split
    return ([pl.BlockSpec((1, rows, w // split), lambda b, j: (b, 0, 0)) for w in widths],
            [jax.ShapeDtypeStruct((m // (nt * tm), rows, w // split), F32) for w in widths])


def _resident(shape):
    nd = len(shape)
    return pl.BlockSpec(shape, lambda *_: (0,) * nd, pipeline_mode=pl.Buffered(1))


def _mla_proj(x, w, gq, gkv, cos, sin, cos_t, sin_t, wk, wv_t, wq_t, metas=None):
    m, d = x.shape
    tm = min(TM, m)
    nt = cos.shape[0] // tm
    row = lambda b, j: (b * nt + j, 0)
    col = lambda b, j: (0, 0, b * nt + j)
    tab = lambda b, j: (j, 0)
    tab_t = lambda b, j: (0, j)
    f32_specs, f32_shapes = _row_specs(m, tm, nt, (KV_LORA, ROPE_A), metas)
    extra = [] if metas is None else list(metas)
    return pl.pallas_call(
        functools.partial(_mla_proj_kernel, with_meta=metas is not None),
        grid=(m // (nt * tm), nt),
        in_specs=[pl.BlockSpec((tm, d), row), _resident(w.shape), _full(gq.shape), _full(gkv.shape),
                  pl.BlockSpec((tm, LANES), tab), pl.BlockSpec((tm, LANES), tab),
                  pl.BlockSpec((ROPE_A // 2, tm), tab_t), pl.BlockSpec((ROPE_A // 2, tm), tab_t),
                  _resident(wk.shape), _resident(wv_t.shape), _resident(wq_t.shape)]
                 + [_full(a.shape) for a in extra],
        out_specs=f32_specs + [pl.BlockSpec((H_A, tm, LANES), lambda b, j: (0, b * nt + j, 0)),
                               pl.BlockSpec((H_A, V_A, tm), col), pl.BlockSpec((H_A, LANES, tm), col)],
        out_shape=f32_shapes + [jax.ShapeDtypeStruct((H_A, m, LANES), BF16),
                                jax.ShapeDtypeStruct((H_A, V_A, m), BF16),
                                jax.ShapeDtypeStruct((H_A, LANES, m), BF16)],
        compiler_params=_params(("parallel", "arbitrary"), 48),
        name="mla_proj",
    )(x, w, gq, gkv, cos, sin, cos_t, sin_t, wk, wv_t, wq_t, *extra)


def _flash_tile(s_t, v_t, m_sc, l_sc, acc_sc, i):
    m_prev = m_sc[i]
    m_new = jnp.maximum(m_prev, jnp.max(s_t, axis=0, keepdims=True))
    a = jnp.exp2(m_prev - m_new)
    p = jnp.exp2(s_t - m_new)
    l_sc[i] = a * l_sc[i] + jnp.sum(p, axis=0, keepdims=True)
    acc_sc[i] = a * acc_sc[i] + _dot(v_t, p.astype(BF16))
    m_sc[i] = m_new


def _flash_reset(m_sc, l_sc, acc_sc):
    m_sc[...] = jnp.full(m_sc.shape, -jnp.inf, F32)
    l_sc[...] = jnp.zeros(l_sc.shape, F32)
    acc_sc[...] = jnp.zeros(acc_sc.shape, F32)


def _flash_prefix(score, value_t, n_maps, m_sc, l_sc, acc_sc, s_sc):
    for i in range(n_maps):
        s_sc[0, i * N_META:(i + 1) * N_META] = score(i)
    s = s_sc[0, :n_maps * N_META].reshape(n_maps, N_META, s_sc.shape[2])
    m = jnp.max(s, axis=1, keepdims=True)
    p = jnp.exp2(s - m)
    m_sc[...] = m
    l_sc[...] = jnp.sum(p, axis=1, keepdims=True)
    p = p.astype(BF16)
    for i in range(n_maps):
        acc_sc[i] = _dot(value_t(i), p[i])


def _causal_pairs(n):
    pairs = [(q, k) for q in range(n) for k in range(q + 1)]
    return (jnp.asarray([p[0] for p in pairs], jnp.int32), jnp.asarray([p[1] for p in pairs], jnp.int32))


def _causal_visible(t):
    return lax.broadcasted_iota(jnp.int32, (t, t), 0) <= lax.broadcasted_iota(jnp.int32, (t, t), 1)


def _mla_attn_kernel(qi_ref, ki_ref, *refs, t, has_prefix):
    if has_prefix:
        q_t_ref, k_ref, v_t_ref, km_ref, vm_t_ref, o_ref, m_sc, l_sc, acc_sc, s_sc = refs
    else:
        q_t_ref, k_ref, v_t_ref, o_ref, m_sc, l_sc, acc_sc, s_sc = refs
    qi, ki = qi_ref[pl.program_id(1)], ki_ref[pl.program_id(1)]

    def all_heads(k_src, v_t_src, visible):
        n_keys = k_src.shape[1]

        def scores(h, slot):
            s_sc[slot, :n_keys] = _dot(k_src[h], q_t_ref[h])

        def update(h, slot):
            s_t = s_sc[slot, :n_keys]
            if visible is not None:
                s_t = jnp.where(visible, s_t, -jnp.inf)
            _flash_tile(s_t, v_t_src[h], m_sc, l_sc, acc_sc, h)

        def head_pair(h, last):
            scores(h + 1, 1)
            update(h, 0)
            if not last:
                scores(h + 2, 0)
            update(h + 1, 1)

        def body(i, c):
            head_pair(2 * i, False)
            return c

        scores(0, 0)
        lax.fori_loop(0, H_A // 2 - 1, body, 0)
        head_pair(H_A - 2, True)

    @pl.when(ki == 0)
    def _init():
        if has_prefix:
            _flash_prefix(lambda h: _dot(km_ref[h], q_t_ref[h]), lambda h: vm_t_ref[h], H_A,
                          m_sc, l_sc, acc_sc, s_sc)
        else:
            _flash_reset(m_sc, l_sc, acc_sc)

    @pl.when(ki < qi)
    def _below_diagonal():
        all_heads(k_ref, v_t_ref, None)

    @pl.when(ki == qi)
    def _diagonal():
        all_heads(k_ref, v_t_ref, _causal_visible(t))
        o_t = (acc_sc[...] / l_sc[...]).reshape(H_A * V_A, t)
        o_ref[...] = o_t.T.astype(BF16)


def _mla_attn(q_t, k, v_t, k_meta, v_meta_t, nb, s, t):
    n = s // t
    has_prefix = k_meta is not None
    qi_tab, ki_tab = _causal_pairs(n)
    q_blk = lambda b, p, qi, ki: b * n + qi[p]
    k_blk = lambda b, p, qi, ki: b * n + ki[p]
    in_specs = [pl.BlockSpec((H_A, LANES, t), lambda *a: (0, 0, q_blk(*a))),
                pl.BlockSpec((H_A, t, LANES), lambda *a: (0, k_blk(*a), 0)),
                pl.BlockSpec((H_A, V_A, t), lambda *a: (0, 0, k_blk(*a)))]
    args = [q_t, k, v_t]
    if has_prefix:
        in_specs += [_full(k_meta.shape), _full(v_meta_t.shape)]
        args += [k_meta, v_meta_t]
    grid_spec = pltpu.PrefetchScalarGridSpec(
        num_scalar_prefetch=2,
        grid=(nb, qi_tab.shape[0]),
        in_specs=in_specs,
        out_specs=pl.BlockSpec((t, H_A * V_A), lambda *a: (q_blk(*a), 0)),
        scratch_shapes=[pltpu.VMEM((H_A, 1, t), F32), pltpu.VMEM((H_A, 1, t), F32),
                        pltpu.VMEM((H_A, V_A, t), F32), pltpu.VMEM((2, t, t), F32)],
    )
    return pl.pallas_call(
        functools.partial(_mla_attn_kernel, t=t, has_prefix=has_prefix),
        grid_spec=grid_spec,
        out_shape=jax.ShapeDtypeStruct((nb * s, H_A * V_A), BF16),
        compiler_params=_params(("parallel", "arbitrary"), 48),
        name="mla_attn",
    )(qi_tab, ki_tab, *args)


def _diff_lambda(lam_ref, lam_init):
    v = lam_ref[...]
    s1 = jnp.sum(v[0:1] * v[1:2], axis=-1, keepdims=True)
    s2 = jnp.sum(v[2:3] * v[3:4], axis=-1, keepdims=True)
    return jnp.exp(s1) - jnp.exp(s2) + lam_init


def _diff_attn_kernel(qi_ref, ki_ref, *refs, t, has_prefix, lam_init):
    if has_prefix:
        q_t_ref, k_ref, v_t_ref, km_ref, vm_t_ref, lam_ref, g_ref, o_ref, m_sc, l_sc, acc_sc, s_sc = refs
    else:
        q_t_ref, k_ref, v_t_ref, lam_ref, g_ref, o_ref, m_sc, l_sc, acc_sc, s_sc = refs
    qi, ki = qi_ref[pl.program_id(1)], ki_ref[pl.program_id(1)]
    hd = 2 * DH_B
    row = lax.broadcasted_iota(jnp.int32, (hd, t), 0)

    def all_maps(k_src, v_t_src, visible):
        n_keys = k_src.shape[0]

        def scores(i, slot):
            h, c = divmod(i, 2)
            n = h // GROUP_B
            q_t = q_t_ref[h]
            q_c = jnp.where((row < DH_B) == (c == 0), q_t, jnp.zeros_like(q_t))
            s_sc[slot, :n_keys] = _dot(k_src[:, n * hd:(n + 1) * hd], q_c)

        def update(i, slot):
            s_t = s_sc[slot, :n_keys]
            if visible is not None:
                s_t = jnp.where(visible, s_t, -jnp.inf)
            _flash_tile(s_t, v_t_src[i // (2 * GROUP_B)], m_sc, l_sc, acc_sc, i)

        n_maps = 2 * H_B
        scores(0, 0)
        for i in range(n_maps):
            if i + 1 < n_maps:
                scores(i + 1, (i + 1) % 2)
            update(i, i % 2)

    @pl.when(ki == 0)
    def _init():
        if has_prefix:
            def score(i):
                h, c = divmod(i, 2)
                n = h // GROUP_B
                q_t = q_t_ref[h]
                q_c = jnp.where((row < DH_B) == (c == 0), q_t, jnp.zeros_like(q_t))
                return _dot(km_ref[:, n * hd:(n + 1) * hd], q_c)

            _flash_prefix(score, lambda i: vm_t_ref[i // (2 * GROUP_B)], 2 * H_B, m_sc, l_sc, acc_sc, s_sc)
        else:
            _flash_reset(m_sc, l_sc, acc_sc)

    @pl.when(ki < qi)
    def _below_diagonal():
        all_maps(k_ref, v_t_ref, None)

    @pl.when(ki == qi)
    def _diagonal():
        all_maps(k_ref, v_t_ref, _causal_visible(t))
        lam = _diff_lambda(lam_ref, lam_init)
        g = g_ref[...] * (1.0 - lam_init)
        outs = []
        for h in range(H_B):
            o = acc_sc[2 * h] / l_sc[2 * h] - lam * (acc_sc[2 * h + 1] / l_sc[2 * h + 1])
            o = o * lax.rsqrt(jnp.mean(o * o, axis=0, keepdims=True) + RMS_EPS)
            outs.append(o * g)
        o_ref[...] = jnp.concatenate(outs, axis=0).T.astype(BF16)


def _diff_attn(q_t, k, v_t, k_meta, v_meta_t, lam_vecs, g_col, lam_init, nb, s, t):
    n = s // t
    has_prefix = k_meta is not None
    hd = 2 * DH_B
    qi_tab, ki_tab = _causal_pairs(n)
    q_blk = lambda b, p, qi, ki: b * n + qi[p]
    k_blk = lambda b, p, qi, ki: b * n + ki[p]
    in_specs = [pl.BlockSpec((H_B, hd, t), lambda *a: (0, 0, q_blk(*a))),
                pl.BlockSpec((t, KVH_B * hd), lambda *a: (k_blk(*a), 0)),
                pl.BlockSpec((KVH_B, hd, t), lambda *a: (0, 0, k_blk(*a)))]
    args = [q_t, k, v_t]
    if has_prefix:
        in_specs += [_full(k_meta.shape), _full(v_meta_t.shape)]
        args += [k_meta, v_meta_t]
    in_specs += [_full(lam_vecs.shape), _full(g_col.shape)]
    args += [lam_vecs, g_col]
    nvh = 2 * H_B
    grid_spec = pltpu.PrefetchScalarGridSpec(
        num_scalar_prefetch=2,
        grid=(nb, qi_tab.shape[0]),
        in_specs=in_specs,
        out_specs=pl.BlockSpec((t, H_B * hd), lambda *a: (q_blk(*a), 0)),
        scratch_shapes=[pltpu.VMEM((nvh, 1, t), F32), pltpu.VMEM((nvh, 1, t), F32),
                        pltpu.VMEM((nvh, hd, t), F32), pltpu.VMEM((2, t, t), F32)],
    )
    return pl.pallas_call(
        functools.partial(_diff_attn_kernel, t=t, has_prefix=has_prefix, lam_init=lam_init),
        grid_spec=grid_spec,
        out_shape=jax.ShapeDtypeStruct((nb * s, H_B * hd), BF16),
        compiler_params=_params(("parallel", "arbitrary"), 48),
        name="diff_attn",
    )(qi_tab, ki_tab, *args)


def _chunk_pipeline(n_chunks_total, start_chunk, wait_chunk, b, per_batch, consume, carry):
    g0 = b * per_batch

    @pl.when(b == 0)
    def _prime():
        for d in range(DEC_NBUF - 1):
            if d < n_chunks_total:
                start_chunk(d, d % DEC_NBUF)

    def body(c, carry):
        g = g0 + c
        slot = g % DEC_NBUF
        wait_chunk(slot)
        nxt = g + DEC_NBUF - 1

        @pl.when(nxt < n_chunks_total)
        def _():
            start_chunk(nxt, nxt % DEC_NBUF)

        return consume(slot, carry)

    return lax.fori_loop(0, per_batch, body, carry)


def _mla_decode_kernel(pt_ref, q_ref, knew_ref, ckv_hbm, kr_hbm, o_ref, ckv_buf, kr_buf, sem_c, sem_r,
                       *, layer, n_pages, cpp, n_batch):
    b = pl.program_id(0)
    per_batch = n_pages // cpp
    total = n_batch * per_batch

    def copies(g, slot, j):
        page = pt_ref[g * cpp + j]
        return (pltpu.make_async_copy(ckv_hbm.at[layer, page], ckv_buf.at[slot, j], sem_c.at[slot]),
                pltpu.make_async_copy(kr_hbm.at[layer, page], kr_buf.at[slot, :, pl.ds(j * PAGE, PAGE)],
                                      sem_r.at[slot]))

    def start_chunk(g, slot):
        for j in range(cpp):
            for cp in copies(g, slot, j):
                cp.start()

    def wait_chunk(slot):
        for j in range(cpp):
            for cp in copies(0, slot, j):
                cp.wait()

    q = q_ref[0]
    qlat, qr = q[:, :KV_LORA], q[:, KV_LORA:KV_LORA + ROPE_A]
    knew = knew_ref[0].astype(F32)
    m0 = jnp.sum(q.astype(F32) * knew, axis=-1, keepdims=True)
    l0 = jnp.ones_like(m0)
    acc0 = jnp.broadcast_to(knew[:, :KV_LORA], (H_A, KV_LORA))

    def consume(slot, carry):
        m_prev, l_prev, acc = carry
        kc = ckv_buf[slot].reshape(cpp * PAGE, KV_LORA).astype(BF16)
        kr_t = kr_buf[slot].astype(BF16)
        s = _dot_nt(qlat, kc) + _dot(qr, kr_t)
        m_new = jnp.maximum(m_prev, jnp.max(s, axis=-1, keepdims=True))
        a = jnp.exp2(m_prev - m_new)
        p = jnp.exp2(s - m_new)
        l_new = a * l_prev + jnp.sum(p, axis=-1, keepdims=True)
        return m_new, l_new, a * acc + _dot(p.astype(BF16), kc)

    _, l, acc = _chunk_pipeline(total, start_chunk, wait_chunk, b, per_batch, consume, (m0, l0, acc0))
    o_ref[0] = acc / l


def _mla_decode(pt_flat, q, knew, cache_ckv, cache_kr_t, layer, n_pages):
    nb = q.shape[0]
    cpp = min(MLA_PAGES_PER_CHUNK, n_pages)
    grid_spec = pltpu.PrefetchScalarGridSpec(
        num_scalar_prefetch=1,
        grid=(nb,),
        in_specs=[pl.BlockSpec((1, H_A, KA_PAD), lambda b, pt: (b, 0, 0)),
                  pl.BlockSpec((1, 1, KA_PAD), lambda b, pt: (b, 0, 0)),
                  pl.BlockSpec(memory_space=pl.ANY), pl.BlockSpec(memory_space=pl.ANY)],
        out_specs=pl.BlockSpec((1, H_A, KV_LORA), lambda b, pt: (b, 0, 0)),
        scratch_shapes=[pltpu.VMEM((DEC_NBUF, cpp, PAGE, KV_LORA), F32),
                        pltpu.VMEM((DEC_NBUF, ROPE_A, cpp * PAGE), F32),
                        pltpu.SemaphoreType.DMA((DEC_NBUF,)), pltpu.SemaphoreType.DMA((DEC_NBUF,))],
    )
    return pl.pallas_call(
        functools.partial(_mla_decode_kernel, layer=layer, n_pages=n_pages, cpp=cpp, n_batch=nb),
        grid_spec=grid_spec,
        out_shape=jax.ShapeDtypeStruct((nb, H_A, KV_LORA), F32),
        compiler_params=_params(("arbitrary",), 32),
        name="mla_decode",
    )(pt_flat, q, knew, cache_ckv, cache_kr_t)


def _mla_uv_kernel(o_ref, wuv_ref, out_ref):
    outs = [_dot(o_ref[h].astype(BF16), wuv_ref[h]) for h in range(H_A)]
    out_ref[...] = jnp.concatenate(outs, axis=1).astype(BF16)


def _mla_uv(olat, wuv):
    m = olat.shape[1]
    return pl.pallas_call(
        _mla_uv_kernel,
        in_specs=[_full(olat.shape), _full(wuv.shape)],
        out_specs=_full((m, H_A * V_A)),
        out_shape=jax.ShapeDtypeStruct((m, H_A * V_A), BF16),
        name="mla_uv",
    )(olat, wuv)


def _diff_decode_kernel(pt_ref, q_ref, knew_ref, vnew_ref, lam_ref, g_ref, k_hbm, v_hbm, o_ref,
                        k_buf, v_buf, sem_k, sem_v, *, layer, n_pages, cpp, n_batch, lam_init):
    b = pl.program_id(0)
    per_batch = n_pages // cpp
    total = n_batch * per_batch
    hd = 2 * DH_B
    prow = PAGE * KVH_B
    rows = 2 * GROUP_B

    def copies(g, slot, j):
        page = pt_ref[g * cpp + j]
        dst = pl.ds(j * prow, prow)
        return (pltpu.make_async_copy(k_hbm.at[layer, page], k_buf.at[slot, dst], sem_k.at[slot]),
                pltpu.make_async_copy(v_hbm.at[layer, page], v_buf.at[slot, dst], sem_v.at[slot]))

    def start_chunk(g, slot):
        for j in range(cpp):
            for cp in copies(g, slot, j):
                cp.start()

    def wait_chunk(slot):
        for j in range(cpp):
            for cp in copies(0, slot, j):
                cp.wait()

    qs = [q_ref[0, n] for n in range(KVH_B)]
    knew = knew_ref[0].astype(F32)
    vnew = vnew_ref[0].astype(F32)
    carry = []
    for n in range(KVH_B):
        m0 = jnp.sum(qs[n].astype(F32) * knew[:, n * hd:(n + 1) * hd], axis=-1, keepdims=True)
        carry += [m0, jnp.ones_like(m0), jnp.broadcast_to(vnew[:, n * hd:(n + 1) * hd], (8, hd))]

    def consume(slot, carry):
        out = []
        for n in range(KVH_B):
            m_prev, l_prev, acc = carry[3 * n:3 * n + 3]
            sel = pl.ds(n, cpp * PAGE, stride=KVH_B)
            kn = k_buf[slot, sel, :].astype(BF16)
            vn = v_buf[slot, sel, :].astype(BF16)
            s = _dot_nt(qs[n], kn)
            m_new = jnp.maximum(m_prev, jnp.max(s, axis=-1, keepdims=True))
            a = jnp.exp2(m_prev - m_new)
            p = jnp.exp2(s - m_new)
            out += [m_new, a * l_prev + jnp.sum(p, axis=-1, keepdims=True), a * acc + _dot(p.astype(BF16), vn)]
        return tuple(out)

    carry = _chunk_pipeline(total, start_chunk, wait_chunk, b, per_batch, consume, tuple(carry))
    lam = _diff_lambda(lam_ref, lam_init)
    heads = []
    for n in range(KVH_B):
        o = carry[3 * n + 2] / carry[3 * n + 1]
        heads.append(o[0:GROUP_B] - lam * o[GROUP_B:rows])
    o = jnp.concatenate(heads, axis=0)
    o_ref[0] = _rms(o, g_ref[...]) * (1.0 - lam_init)


def _diff_decode(pt_flat, q, knew, vnew, lam_vecs, g_sub, cache_k, cache_v, layer, n_pages, lam_init):
    nb = q.shape[0]
    cpp = min(DIFF_PAGES_PER_CHUNK, n_pages)
    hd = 2 * DH_B
    kw = KVH_B * hd
    grid_spec = pltpu.PrefetchScalarGridSpec(
        num_scalar_prefetch=1,
        grid=(nb,),
        in_specs=[pl.BlockSpec((1, KVH_B, 8, hd), lambda b, pt: (b, 0, 0, 0)),
                  pl.BlockSpec((1, 1, kw), lambda b, pt: (b, 0, 0)),
                  pl.BlockSpec((1, 1, kw), lambda b, pt: (b, 0, 0)),
                  pl.BlockSpec(lam_vecs.shape, lambda b, pt: (0, 0)),
                  pl.BlockSpec(g_sub.shape, lambda b, pt: (0, 0)),
                  pl.BlockSpec(memory_space=pl.ANY), pl.BlockSpec(memory_space=pl.ANY)],
        out_specs=pl.BlockSpec((1, H_B, hd), lambda b, pt: (b, 0, 0)),
        scratch_shapes=[pltpu.VMEM((DEC_NBUF, cpp * PAGE * KVH_B, hd), F32),
                        pltpu.VMEM((DEC_NBUF, cpp * PAGE * KVH_B, hd), F32),
                        pltpu.SemaphoreType.DMA((DEC_NBUF,)), pltpu.SemaphoreType.DMA((DEC_NBUF,))],
    )
    return pl.pallas_call(
        functools.partial(_diff_decode_kernel, layer=layer, n_pages=n_pages, cpp=cpp, n_batch=nb,
                          lam_init=lam_init),
        grid_spec=grid_spec,
        out_shape=jax.ShapeDtypeStruct((nb, H_B, hd), F32),
        compiler_params=_params(("arbitrary",), 40),
        name="diff_decode",
    )(pt_flat, q, knew, vnew, lam_vecs, g_sub, cache_k, cache_v)


def _diff_in_kernel(*refs, with_meta):
    x_ref, wkv_ref, wq_t_ref, wv_t_ref, cos_ref, sin_ref, cos_t_ref, sin_t_ref = refs[:8]
    meta_refs = refs[8:10] if with_meta else None
    k_ref, v_ref, kb_ref, q_t_ref, v_t_ref = refs[-5:]
    xb = x_ref[...].astype(BF16)
    tm = xb.shape[0]
    nk = KVH_B * 2 * DH_B
    hd = 2 * DH_B
    y = _dot(xb, wkv_ref[...])
    cos, sin = cos_ref[...], sin_ref[...]
    parts = []
    for c0 in range(0, nk, LANES):
        sl = y[:, c0:c0 + LANES]
        parts.append(sl * cos + _rot_half(sl, ROT_B // 2) * sin)
    k = jnp.concatenate(parts, axis=1)
    _store_rows([(k_ref, k), (v_ref, y[:, nk:])], meta_refs)
    kb_ref[...] = k.astype(BF16)
    v_t_ref[...] = _dot_nt(wv_t_ref[...], xb).reshape(KVH_B, hd, tm).astype(BF16)
    q_t = _dot_nt(wq_t_ref[...], xb)
    cos_t, sin_t = cos_t_ref[...], sin_t_ref[...]
    half = ROT_B // 2
    pieces = []
    for b0 in range(0, H_B * hd, DH_B):
        r1, r2 = _rope_rows(q_t[b0:b0 + half], q_t[b0 + half:b0 + 2 * half], cos_t, sin_t)
        pieces += [r1, r2, q_t[b0 + 2 * half:b0 + DH_B]]
    q_t_ref[...] = (jnp.concatenate(pieces, axis=0) * SCALE_B).reshape(H_B, hd, tm).astype(BF16)
    _store_meta_rows((k_ref, v_ref), meta_refs)


def _diff_in(x, wkv, wq_t, wv_t, cos, sin, cos_t, sin_t, metas=None):
    m, d = x.shape
    tm = min(TM, m)
    nt = cos.shape[0] // tm
    nk = KVH_B * 2 * DH_B
    hd = 2 * DH_B
    row = lambda b, j: (b * nt + j, 0)
    col = lambda b, j: (0, 0, b * nt + j)
    tab = lambda b, j: (j, 0)
    tab_t = lambda b, j: (0, j)
    f32_specs, f32_shapes = _row_specs(m, tm, nt, (nk, nk), metas, split=KVH_B)
    extra = [] if metas is None else list(metas)
    return pl.pallas_call(
        functools.partial(_diff_in_kernel, with_meta=metas is not None),
        grid=(m // (nt * tm), nt),
        in_specs=[pl.BlockSpec((tm, d), row), _resident(wkv.shape), _resident(wq_t.shape), _resident(wv_t.shape),
                  pl.BlockSpec((tm, LANES), tab), pl.BlockSpec((tm, LANES), tab),
                  pl.BlockSpec((ROT_B // 2, tm), tab_t), pl.BlockSpec((ROT_B // 2, tm), tab_t)]
                 + [_full(a.shape) for a in extra],
        out_specs=f32_specs + [pl.BlockSpec((tm, nk), row), pl.BlockSpec((H_B, hd, tm), col),
                               pl.BlockSpec((KVH_B, hd, tm), col)],
        out_shape=f32_shapes + [jax.ShapeDtypeStruct((m, nk), BF16), jax.ShapeDtypeStruct((H_B, hd, m), BF16),
                                jax.ShapeDtypeStruct((KVH_B, hd, m), BF16)],
        compiler_params=_params(("parallel", "arbitrary"), 52),
        name="diff_in",
    )(x, wkv, wq_t, wv_t, cos, sin, cos_t, sin_t, *extra)


def _proj_ln_kernel(o_ref, w_ref, x_ref, g_ref, b_ref, out_ref, *, alpha):
    mix = _dot(o_ref[...], w_ref[...])
    out_ref[...] = _layernorm(alpha * x_ref[...] + mix, g_ref[...], b_ref[...])


def _proj_ln(o, w, x, g, b, alpha):
    m, d = x.shape
    tm = min(TM, m)
    row = lambda i: (i, 0)
    return pl.pallas_call(
        functools.partial(_proj_ln_kernel, alpha=alpha),
        grid=(m // tm,),
        in_specs=[pl.BlockSpec((tm, o.shape[1]), row), _full(w.shape), pl.BlockSpec((tm, d), row),
                  _full(g.shape), _full(b.shape)],
        out_specs=pl.BlockSpec((tm, d), row),
        out_shape=jax.ShapeDtypeStruct((m, d), F32),
        compiler_params=_params(("parallel",), 32),
        name="proj_ln",
    )(o, w, x, g, b)


def _ffn_kernel(x_ref, wg_ref, wu_ref, wd_ref, g_ref, b_ref, out_ref, acc_ref, *, alpha, d_ff):
    x = x_ref[...]
    xb = x.astype(BF16)
    for i, c0 in enumerate(range(0, d_ff, FF_CHUNK)):
        gate = _dot(xb, wg_ref[:, c0:c0 + FF_CHUNK])
        up = _dot(xb, wu_ref[:, c0:c0 + FF_CHUNK])
        act = (gate * jax.nn.sigmoid(gate) * up).astype(BF16)
        part = _dot(act, wd_ref[c0:c0 + FF_CHUNK, :])
        if i == 0:
            acc_ref[...] = part
        else:
            acc_ref[...] += part
    out_ref[...] = _layernorm(alpha * x + acc_ref[...], g_ref[...], b_ref[...])


def _ffn(x, wg, wu, wd, g, b, alpha):
    m, d = x.shape
    d_ff = wg.shape[1]
    tm = min(TM, m)
    row = lambda i: (i, 0)
    return pl.pallas_call(
        functools.partial(_ffn_kernel, alpha=alpha, d_ff=d_ff),
        grid=(m // tm,),
        in_specs=[pl.BlockSpec((tm, d), row), _resident(wg.shape), _resident(wu.shape), _resident(wd.shape),
                  _full(g.shape), _full(b.shape)],
        out_specs=pl.BlockSpec((tm, d), row),
        out_shape=jax.ShapeDtypeStruct((m, d), F32),
        scratch_shapes=[pltpu.VMEM((tm, d), F32)],
        compiler_params=_params(("parallel",), 48),
        name="ffn",
    )(x, wg, wu, wd, g, b)


def _rope_angles(pos, d):
    inv = jnp.power(ROPE_THETA, -jnp.arange(0, d, 2, dtype=F32) / d)
    ang = pos.astype(F32)[:, None] * inv[None, :]
    return jnp.cos(ang), jnp.sin(ang)


def _rope_tables(pos, d, period):
    cos, sin = _rope_angles(pos, d)
    n = pos.shape[0]
    cos_row = jnp.concatenate([cos, cos, jnp.ones((n, period - d), F32)], axis=1)
    sin_row = jnp.concatenate([sin, sin, jnp.zeros((n, period - d), F32)], axis=1)
    reps = LANES // period
    return jnp.tile(cos_row, (1, reps)), jnp.tile(sin_row, (1, reps)), cos.T, sin.T


def kernel(x_prompt, x_sample, cache_mla_ckv, cache_mla_krope, cache_diff_k, cache_diff_v, page_table, meta_tokens, mla_w_in, mla_g_q, mla_g_kv, mla_w_uq, mla_w_uk, mla_w_uv, mla_w_o, diff_w_in, diff_lam_q1, diff_lam_k1, diff_lam_q2, diff_lam_k2, diff_g_sub, diff_w_o, ffn_w_gu, ffn_w_down, ln1_g, ln1_b, ln2_g, ln2_b):
    nb, seq, d = x_prompt.shape
    ndec, dec_seq, _ = x_sample.shape
    assert dec_seq == 1 and meta_tokens.shape[0] == N_META
    n_small = N_META + ndec
    assert n_small <= SMALL_PAD and seq % min(ATT_T, seq) == 0
    n_pages = page_table.shape[1]
    past = n_pages * PAGE
    depth = ln1_g.shape[0]
    alpha = (2 * depth) ** 0.25
    d_ff = ffn_w_down.shape[1]
    hd = 2 * DH_B
    t_att = min(ATT_T, seq)

    hr = x_prompt.reshape(nb * seq, d)
    hs = jnp.concatenate([meta_tokens.astype(F32), x_sample.reshape(ndec, d)], axis=0)
    pos_r = N_META + jnp.arange(seq, dtype=jnp.int32)
    pos_s = jnp.concatenate([jnp.arange(N_META, dtype=jnp.int32), jnp.full((SMALL_PAD - N_META,), past, jnp.int32)])
    pt_flat = page_table.reshape(-1).astype(jnp.int32)
    row2 = lambda a: a.reshape(1, -1).astype(F32)
    pad_small = lambda a: jnp.pad(a, ((0, SMALL_PAD - n_small), (0, 0)))
    dec = slice(N_META, n_small)

    outs = {k: [] for k in ("p_ckv", "p_kr", "p_k", "p_v", "s_ckv", "s_kr", "s_k", "s_v")}

    for i in range(depth):
        j = i // 2
        if i % 2 == 0:
            tr = _rope_tables(pos_r, ROPE_A, ROPE_A)
            ts = _rope_tables(pos_s, ROPE_A, ROPE_A)
            w_in = jnp.pad(mla_w_in[j], ((0, 0), (0, LANES - ROPE_A))).astype(BF16)
            wq = mla_w_uq[j].reshape(Q_LORA, H_A, NOPE_A + ROPE_A)
            wuq = jnp.concatenate([wq[:, :, :NOPE_A].reshape(Q_LORA, H_A * NOPE_A),
                                   wq[:, :, NOPE_A:].reshape(Q_LORA, H_A * ROPE_A)], axis=1).astype(BF16)
            wuk_t = jnp.swapaxes(mla_w_uk[j], 1, 2).astype(BF16)
            wuv = mla_w_uv[j].astype(BF16)
            wk = jnp.pad(mla_w_uk[j], ((0, 0), (0, 0), (0, LANES - NOPE_A)))
            wk = jnp.swapaxes(wk, 0, 1).reshape(KV_LORA, H_A * LANES).astype(BF16)
            wv_t = jnp.swapaxes(mla_w_uv[j], 1, 2).reshape(H_A * V_A, KV_LORA).astype(BF16)
            wq_t = jnp.pad(wq, ((0, 0), (0, 0), (0, LANES - NOPE_A - ROPE_A)))
            wq_t = jnp.transpose(wq_t, (1, 2, 0)).reshape(H_A * LANES, Q_LORA).astype(BF16)
            gq, gkv = row2(mla_g_q[j]), row2(mla_g_kv[j])

            _, _, k_s, vt_s, qt_s = _mla_proj(pad_small(hs), w_in, gq, gkv, *ts, wk, wv_t, wq_t)
            cq_s, ckv_s, kr_s, kb_s = _mla_in(hs, w_in, gq, gkv, ts[0][:n_small], ts[1][:n_small])
            ckv_r, kr_r, k_r, vt_r, qt_r = _mla_proj(hr, w_in, gq, gkv, *tr, wk, wv_t, wq_t,
                                                     metas=(ckv_s[:N_META], kr_s[:N_META]))
            q_s = _mla_q(cq_s, wuq, wuk_t, ts[0][:n_small], ts[1][:n_small])

            k_m, vt_m, qt_m = k_s[:, :LANES], vt_s[:, :, :LANES], qt_s[:, :, :LANES]
            o_r = _mla_attn(qt_r, k_r, vt_r, k_s[:, :N_META], vt_s[:, :, :N_META], nb, seq, t_att)
            o_m = _mla_attn(qt_m, k_m, vt_m, None, None, 1, LANES, LANES)[:N_META]
            olat = _mla_decode(pt_flat, jnp.swapaxes(q_s[:, dec], 0, 1), kb_s[dec].reshape(ndec, 1, KA_PAD),
                               cache_mla_ckv, jnp.swapaxes(cache_mla_krope, 2, 3), j, n_pages)
            o_d = _mla_uv(jnp.swapaxes(olat, 0, 1), wuv)
            w_o = mla_w_o[j].astype(BF16)

            outs["p_ckv"].append(ckv_r)
            outs["p_kr"].append(kr_r)
            outs["s_ckv"].append(ckv_s[dec].reshape(ndec, 1, KV_LORA))
            outs["s_kr"].append(kr_s[dec].reshape(ndec, 1, ROPE_A))
        else:
            lam_init = 0.8 - 0.6 * math.exp(-0.3 * i)
            tr = _rope_tables(pos_r, ROT_B, DH_B)
            ts = _rope_tables(pos_s, ROT_B, DH_B)
            nq, nk = H_B * hd, KVH_B * hd
            w = diff_w_in[j]
            wkv = w[:, nq:].astype(BF16)
            wq_t = w[:, :nq].T.astype(BF16)
            wv_t = w[:, nq + nk:].T.astype(BF16)
            lam_vecs = jnp.stack([diff_lam_q1[j], diff_lam_k1[j], diff_lam_q2[j], diff_lam_k2[j]]).astype(F32)
            g_sub = row2(diff_g_sub[j])

            k_s, v_s, kb_s, qt_s, vt_s = _diff_in(pad_small(hs), wkv, wq_t, wv_t, *ts)
            meta_kv = (k_s[:N_META].reshape(N_META * KVH_B, hd), v_s[:N_META].reshape(N_META * KVH_B, hd))
            k_r, v_r, kb_r, qt_r, vt_r = _diff_in(hr, wkv, wq_t, wv_t, *tr, metas=meta_kv)

            k_m, vt_m, qt_m = kb_s[:LANES], vt_s[:, :, :LANES], qt_s[:, :, :LANES]
            o_r = _diff_attn(qt_r, kb_r, vt_r, kb_s[:N_META], vt_s[:, :, :N_META], lam_vecs, g_sub.T, lam_init,
                             nb, seq, t_att)
            o_m = _diff_attn(qt_m, k_m, vt_m, None, None, lam_vecs, g_sub.T, lam_init, 1, LANES, LANES)[:N_META]
            qd = jnp.transpose(qt_s[:, :, dec], (2, 0, 1)).reshape(ndec, KVH_B, GROUP_B, 2, DH_B)
            qd = jnp.swapaxes(qd, 2, 3)
            eye = jnp.eye(2, dtype=BF16)
            qd = (qd[:, :, :, :, None, :] * eye[None, None, :, None, :, None]).reshape(ndec, KVH_B, 2 * GROUP_B, hd)
            qd = jnp.pad(qd, ((0, 0), (0, 0), (0, 8 - 2 * GROUP_B), (0, 0)))
            n_pool = cache_diff_k.shape[1]
            ck = cache_diff_k.reshape(cache_diff_k.shape[0], n_pool, PAGE * KVH_B, hd)
            cv = cache_diff_v.reshape(cache_diff_v.shape[0], n_pool, PAGE * KVH_B, hd)
            o_d = _diff_decode(pt_flat, qd, kb_s[dec].reshape(ndec, 1, nk),
                               v_s[dec].astype(BF16).reshape(ndec, 1, nk), lam_vecs, g_sub, ck, cv, j, n_pages,
                               lam_init)
            o_d = o_d.reshape(ndec, H_B * hd).astype(BF16)
            w_o = diff_w_o[j].astype(BF16)

            outs["p_k"].append(k_r.reshape(nb, N_META + seq, KVH_B, hd))
            outs["p_v"].append(v_r.reshape(nb, N_META + seq, KVH_B, hd))
            outs["s_k"].append(k_s[dec].reshape(ndec, 1, KVH_B, hd))
            outs["s_v"].append(v_s[dec].reshape(ndec, 1, KVH_B, hd))

        o_s = jnp.concatenate([o_m, o_d], axis=0)
        g1, b1, g2, b2 = row2(ln1_g[i]), row2(ln1_b[i]), row2(ln2_g[i]), row2(ln2_b[i])
        hr = _proj_ln(o_r, w_o, hr, g1, b1, alpha)
        hs = _proj_ln(o_s, w_o, hs, g1, b1, alpha)
        wg = ffn_w_gu[i][:, :d_ff].astype(BF16)
        wu = ffn_w_gu[i][:, d_ff:].astype(BF16)
        wd = ffn_w_down[i].astype(BF16)
        hr = _ffn(hr, wg, wu, wd, g2, b2, alpha)
        hs = _ffn(hs, wg, wu, wd, g2, b2, alpha)

    y_prompt = hr.reshape(nb, seq, d)
    y_sample = hs[dec].reshape(ndec, 1, d)
    st = lambda k: outs[k][0][None] if len(outs[k]) == 1 else jnp.stack(outs[k], axis=0)
    return (y_prompt, y_sample, st("p_ckv"), st("p_kr"), st("p_k"), st("p_v"),
            st("s_ckv"), st("s_kr"), st("s_k"), st("s_v"))
```

```python
import functools
import math

import jax
import jax.numpy as jnp
from jax import lax
from jax.experimental import pallas as pl
from jax.experimental.pallas import tpu as pltpu

F32 = jnp.float32
BF16 = jnp.bfloat16

N_META = 16
PAGE = 128
ROPE_THETA = 500000.0
LN_EPS = 1e-5
RMS_EPS = 1e-6
H_A, NOPE_A, ROPE_A, V_A = 16, 64, 32, 64
Q_LORA, KV_LORA = 512, 256
LOG2E = math.log2(math.e)
SCALE_A = (NOPE_A + ROPE_A) ** -0.5 * LOG2E
H_B, KVH_B, DH_B = 8, 4, 64
GROUP_B = H_B // KVH_B
ROT_B = DH_B // 4
SCALE_B = DH_B ** -0.5 * LOG2E
LANES = 128
KA_PAD = KV_LORA + LANES

TM = 512
ATT_T = 512
SMALL_PAD = 256
FF_CHUNK = 256
MLA_PAGES_PER_CHUNK = 16
DIFF_PAGES_PER_CHUNK = 8
DEC_NBUF = 3
MIB = 1024 * 1024


def _params(sem, vmem_mib):
    return pltpu.CompilerParams(dimension_semantics=sem, vmem_limit_bytes=vmem_mib * MIB)


def _full(shape):
    nd = len(shape)
    return pl.BlockSpec(shape, lambda *_: (0,) * nd)


def _dot(a, b):
    return jnp.dot(a, b, preferred_element_type=F32)


def _dot_nt(a, b):
    return lax.dot_general(a, b, (((1,), (1,)), ((), ())), preferred_element_type=F32)


def _rms(x, g):
    return x * lax.rsqrt(jnp.mean(x * x, axis=-1, keepdims=True) + RMS_EPS) * g


def _layernorm(x, g, b):
    mu = jnp.mean(x, axis=-1, keepdims=True)
    xc = x - mu
    var = jnp.mean(xc * xc, axis=-1, keepdims=True)
    return xc * lax.rsqrt(var + LN_EPS) * g + b


def _rot_half(x, half):
    lane = lax.broadcasted_iota(jnp.int32, x.shape, 1)
    first = (lane % (2 * half)) < half
    return jnp.where(first, -pltpu.roll(x, LANES - half, 1), pltpu.roll(x, half, 1))


def _rope_rows(x1, x2, cos_t, sin_t):
    return x1 * cos_t - x2 * sin_t, x2 * cos_t + x1 * sin_t


def _mla_latent(x_ref, w_ref, gq_ref, gkv_ref, cos_ref, sin_ref):
    y = _dot(x_ref[...].astype(BF16), w_ref[...])
    cqn = _rms(y[:, :Q_LORA], gq_ref[...]).astype(BF16)
    ckv = _rms(y[:, Q_LORA:Q_LORA + KV_LORA], gkv_ref[...])
    kr = y[:, Q_LORA + KV_LORA:]
    kr = kr * cos_ref[...] + _rot_half(kr, ROPE_A // 2) * sin_ref[...]
    return cqn, ckv, kr


def _mla_in_kernel(x_ref, w_ref, gq_ref, gkv_ref, cos_ref, sin_ref, cq_ref, ckv_ref, kr_ref, kb_ref):
    cqn, ckv, kr = _mla_latent(x_ref, w_ref, gq_ref, gkv_ref, cos_ref, sin_ref)
    cq_ref[...] = cqn
    ckv_ref[...] = ckv
    kr_ref[...] = kr[:, :ROPE_A]
    kb_ref[...] = jnp.concatenate([ckv, kr], axis=1).astype(BF16)


def _mla_in(x, w, gq, gkv, cos, sin):
    m, d = x.shape
    return pl.pallas_call(
        _mla_in_kernel,
        in_specs=[_full(a.shape) for a in (x, w, gq, gkv, cos, sin)],
        out_specs=[_full((m, Q_LORA)), _full((m, KV_LORA)), _full((m, ROPE_A)), _full((m, KA_PAD))],
        out_shape=[jax.ShapeDtypeStruct((m, Q_LORA), BF16), jax.ShapeDtypeStruct((m, KV_LORA), F32),
                   jax.ShapeDtypeStruct((m, ROPE_A), F32), jax.ShapeDtypeStruct((m, KA_PAD), BF16)],
        compiler_params=_params(None, 32),
        name="mla_in",
    )(x, w, gq, gkv, cos, sin)


def _mla_q_kernel(cq_ref, wuq_ref, wuk_ref, cos_ref, sin_ref, q_ref):
    y = _dot(cq_ref[...], wuq_ref[...])
    cos, sin = cos_ref[...], sin_ref[...]
    lane = lax.broadcasted_iota(jnp.int32, cos.shape, 1)
    heads_per_slab = LANES // ROPE_A
    for slab in range(H_A // heads_per_slab):
        base = H_A * NOPE_A + slab * LANES
        sl = y[:, base:base + LANES]
        ro = (sl * cos + _rot_half(sl, ROPE_A // 2) * sin) * SCALE_A
        for k in range(heads_per_slab):
            h = slab * heads_per_slab + k
            piece = ro if k == 0 else pltpu.roll(ro, LANES - ROPE_A * k, 1)
            piece = jnp.where(lane < ROPE_A, piece, 0.0)
            qlat = _dot(y[:, h * NOPE_A:(h + 1) * NOPE_A].astype(BF16), wuk_ref[h]) * SCALE_A
            q_ref[h] = jnp.concatenate([qlat, piece], axis=1).astype(BF16)


def _mla_q(cq, wuq, wuk, cos, sin):
    m = cq.shape[0]
    return pl.pallas_call(
        _mla_q_kernel,
        in_specs=[_full(a.shape) for a in (cq, wuq, wuk, cos, sin)],
        out_specs=_full((H_A, m, KA_PAD)),
        out_shape=jax.ShapeDtypeStruct((H_A, m, KA_PAD), BF16),
        compiler_params=_params(None, 32),
        name="mla_q",
    )(cq, wuq, wuk, cos, sin)


def _store_rows(refs_and_rows, meta_refs):
    for ref, rows in refs_and_rows:
        tm, width = rows.shape
        split = width // ref.shape[-1]
        if meta_refs is None:
            ref[...] = rows
        elif split == 1:
            ref[0, pl.ds(pl.multiple_of(N_META + pl.program_id(1) * tm, 8), tm)] = rows
        else:
            base = pl.multiple_of((N_META + pl.program_id(1) * tm) * split, 8)
            for n in range(split):
                ref[0, pl.ds(base + n, tm, stride=split)] = rows[:, n * ref.shape[-1]:(n + 1) * ref.shape[-1]]


def _store_meta_rows(refs, meta_refs):
    if meta_refs is None:
        return

    @pl.when(pl.program_id(1) == 0)
    def _():
        for ref, meta in zip(refs, meta_refs):
            ref[0, :meta.shape[0]] = meta[...]


def _mla_proj_kernel(*refs, with_meta):
    (x_ref, w_ref, gq_ref, gkv_ref, cos_ref, sin_ref, cos_t_ref, sin_t_ref, wk_ref, wv_t_ref, wq_t_ref) = refs[:11]
    meta_refs = refs[11:13] if with_meta else None
    ckv_ref, kr_ref, k_ref, v_t_ref, q_t_ref = refs[-5:]
    cqn, ckv, kr = _mla_latent(x_ref, w_ref, gq_ref, gkv_ref, cos_ref, sin_ref)
    _store_rows([(ckv_ref, ckv), (kr_ref, kr[:, :ROPE_A])], meta_refs)
    ckv_b = ckv.astype(BF16)
    tm = ckv.shape[0]
    k_rope = pltpu.roll(kr, NOPE_A, 1)
    k_nope = _dot(ckv_b, wk_ref[...])
    for h in range(H_A):
        k_ref[h] = (k_nope[:, h * LANES:(h + 1) * LANES] + k_rope).astype(BF16)
    v_t_ref[...] = _dot_nt(wv_t_ref[...], ckv_b).reshape(H_A, V_A, tm).astype(BF16)
    q_t = _dot_nt(wq_t_ref[...], cqn)
    cos_t, sin_t = cos_t_ref[...], sin_t_ref[...]
    half = ROPE_A // 2
    for h in range(H_A):
        b0 = h * LANES
        r0 = b0 + NOPE_A
        r1, r2 = _rope_rows(q_t[r0:r0 + half], q_t[r0 + half:r0 + 2 * half], cos_t, sin_t)
        piece = jnp.concatenate([q_t[b0:r0], r1, r2, q_t[r0 + 2 * half:b0 + LANES]], axis=0)
        q_t_ref[h] = (piece * SCALE_A).astype(BF16)
    _store_meta_rows((ckv_ref, kr_ref), meta_refs)


def _row_specs(m, tm, nt, widths, metas, split=1):
    if metas is None:
        return ([pl.BlockSpec((tm, w), lambda b, j: (b * nt + j, 0)) for w in widths],
                [jax.ShapeDtypeStruct((m, w), F32) for w in widths])
    rows = (N_META + nt * tm) * split
    return ([pl.BlockSpec((1, rows, w // split), lambda b, j: (b, 0, 0)) for w in widths],
            [jax.ShapeDtypeStruct((m // (nt * tm), rows, w // split), F32) for w in widths])


def _resident(shape):
    nd = len(shape)
    return pl.BlockSpec(shape, lambda *_: (0,) * nd, pipeline_mode=pl.Buffered(1))


def _mla_proj(x, w, gq, gkv, cos, sin, cos_t, sin_t, wk, wv_t, wq_t, metas=None):
    m, d = x.shape
    tm = min(TM, m)
    nt = cos.shape[0] // tm
    row = lambda b, j: (b * nt + j, 0)
    col = lambda b, j: (0, 0, b * nt + j)
    tab = lambda b, j: (j, 0)
    tab_t = lambda b, j: (0, j)
    f32_specs, f32_shapes = _row_specs(m, tm, nt, (KV_LORA, ROPE_A), metas)
    extra = [] if metas is None else list(metas)
    return pl.pallas_call(
        functools.partial(_mla_proj_kernel, with_meta=metas is not None),
        grid=(m // (nt * tm), nt),
        in_specs=[pl.BlockSpec((tm, d), row), _resident(w.shape), _full(gq.shape), _full(gkv.shape),
                  pl.BlockSpec((tm, LANES), tab), pl.BlockSpec((tm, LANES), tab),
                  pl.BlockSpec((ROPE_A // 2, tm), tab_t), pl.BlockSpec((ROPE_A // 2, tm), tab_t),
                  _resident(wk.shape), _resident(wv_t.shape), _resident(wq_t.shape)]
                 + [_full(a.shape) for a in extra],
        out_specs=f32_specs + [pl.BlockSpec((H_A, tm, LANES), lambda b, j: (0, b * nt + j, 0)),
                               pl.BlockSpec((H_A, V_A, tm), col), pl.BlockSpec((H_A, LANES, tm), col)],
        out_shape=f32_shapes + [jax.ShapeDtypeStruct((H_A, m, LANES), BF16),
                                jax.ShapeDtypeStruct((H_A, V_A, m), BF16),
                                jax.ShapeDtypeStruct((H_A, LANES, m), BF16)],
        compiler_params=_params(("parallel", "arbitrary"), 48),
        name="mla_proj",
    )(x, w, gq, gkv, cos, sin, cos_t, sin_t, wk, wv_t, wq_t, *extra)


def _flash_tile(s_t, v_t, m_sc, l_sc, acc_sc, i):
    m_prev = m_sc[i]
    m_new = jnp.maximum(m_prev, jnp.max(s_t, axis=0, keepdims=True))
    a = jnp.exp2(m_prev - m_new)
    p = jnp.exp2(s_t - m_new)
    l_sc[i] = a * l_sc[i] + jnp.sum(p, axis=0, keepdims=True)
    acc_sc[i] = a * acc_sc[i] + _dot(v_t, p.astype(BF16))
    m_sc[i] = m_new


def _flash_reset(m_sc, l_sc, acc_sc):
    m_sc[...] = jnp.full(m_sc.shape, -jnp.inf, F32)
    l_sc[...] = jnp.zeros(l_sc.shape, F32)
    acc_sc[...] = jnp.zeros(acc_sc.shape, F32)


def _flash_prefix(score, value_t, n_maps, m_sc, l_sc, acc_sc, s_sc):
    for i in range(n_maps):
        s_sc[0, i * N_META:(i + 1) * N_META] = score(i)
    s = s_sc[0, :n_maps * N_META].reshape(n_maps, N_META, s_sc.shape[2])
    m = jnp.max(s, axis=1, keepdims=True)
    p = jnp.exp2(s - m)
    m_sc[...] = m
    l_sc[...] = jnp.sum(p, axis=1, keepdims=True)
    p = p.astype(BF16)
    for i in range(n_maps):
        acc_sc[i] = _dot(value_t(i), p[i])


def _causal_pairs(n):
    pairs = [(q, k) for q in range(n) for k in range(q + 1)]
    return (jnp.asarray([p[0] for p in pairs], jnp.int32), jnp.asarray([p[1] for p in pairs], jnp.int32))


def _causal_visible(t):
    return lax.broadcasted_iota(jnp.int32, (t, t), 0) <= lax.broadcasted_iota(jnp.int32, (t, t), 1)


def _mla_attn_kernel(qi_ref, ki_ref, *refs, t, has_prefix):
    if has_prefix:
        q_t_ref, k_ref, v_t_ref, km_ref, vm_t_ref, o_ref, m_sc, l_sc, acc_sc, s_sc = refs
    else:
        q_t_ref, k_ref, v_t_ref, o_ref, m_sc, l_sc, acc_sc, s_sc = refs
    qi, ki = qi_ref[pl.program_id(1)], ki_ref[pl.program_id(1)]

    def all_heads(k_src, v_t_src, visible):
        n_keys = k_src.shape[1]

        def scores(h, slot):
            s_sc[slot, :n_keys] = _dot(k_src[h], q_t_ref[h])

        def update(h, slot):
            s_t = s_sc[slot, :n_keys]
            if visible is not None:
                s_t = jnp.where(visible, s_t, -jnp.inf)
            _flash_tile(s_t, v_t_src[h], m_sc, l_sc, acc_sc, h)

        def head_pair(h, last):
            scores(h + 1, 1)
            update(h, 0)
            if not last:
                scores(h + 2, 0)
            update(h + 1, 1)

        def body(i, c):
            head_pair(2 * i, False)
            return c

        scores(0, 0)
        lax.fori_loop(0, H_A // 2 - 1, body, 0)
        head_pair(H_A - 2, True)

    @pl.when(ki == 0)
    def _init():
        if has_prefix:
            _flash_prefix(lambda h: _dot(km_ref[h], q_t_ref[h]), lambda h: vm_t_ref[h], H_A,
                          m_sc, l_sc, acc_sc, s_sc)
        else:
            _flash_reset(m_sc, l_sc, acc_sc)

    @pl.when(ki < qi)
    def _below_diagonal():
        all_heads(k_ref, v_t_ref, None)

    @pl.when(ki == qi)
    def _diagonal():
        all_heads(k_ref, v_t_ref, _causal_visible(t))
        o_t = (acc_sc[...] / l_sc[...]).reshape(H_A * V_A, t)
        o_ref[...] = o_t.T.astype(BF16)


def _mla_attn(q_t, k, v_t, k_meta, v_meta_t, nb, s, t):
    n = s // t
    has_prefix = k_meta is not None
    qi_tab, ki_tab = _causal_pairs(n)
    q_blk = lambda b, p, qi, ki: b * n + qi[p]
    k_blk = lambda b, p, qi, ki: b * n + ki[p]
    in_specs = [pl.BlockSpec((H_A, LANES, t), lambda *a: (0, 0, q_blk(*a))),
                pl.BlockSpec((H_A, t, LANES), lambda *a: (0, k_blk(*a), 0)),
                pl.BlockSpec((H_A, V_A, t), lambda *a: (0, 0, k_blk(*a)))]
    args = [q_t, k, v_t]
    if has_prefix:
        in_specs += [_full(k_meta.shape), _full(v_meta_t.shape)]
        args += [k_meta, v_meta_t]
    grid_spec = pltpu.PrefetchScalarGridSpec(
        num_scalar_prefetch=2,
        grid=(nb, qi_tab.shape[0]),
        in_specs=in_specs,
        out_specs=pl.BlockSpec((t, H_A * V_A), lambda *a: (q_blk(*a), 0)),
        scratch_shapes=[pltpu.VMEM((H_A, 1, t), F32), pltpu.VMEM((H_A, 1, t), F32),
                        pltpu.VMEM((H_A, V_A, t), F32), pltpu.VMEM((2, t, t), F32)],
    )
    return pl.pallas_call(
        functools.partial(_mla_attn_kernel, t=t, has_prefix=has_prefix),
        grid_spec=grid_spec,
        out_shape=jax.ShapeDtypeStruct((nb * s, H_A * V_A), BF16),
        compiler_params=_params(("parallel", "arbitrary"), 48),
        name="mla_attn",
    )(qi_tab, ki_tab, *args)


def _diff_lambda(lam_ref, lam_init):
    v = lam_ref[...]
    s1 = jnp.sum(v[0:1] * v[1:2], axis=-1, keepdims=True)
    s2 = jnp.sum(v[2:3] * v[3:4], axis=-1, keepdims=True)
    return jnp.exp(s1) - jnp.exp(s2) + lam_init


def _diff_attn_kernel(qi_ref, ki_ref, *refs, t, has_prefix, lam_init):
    if has_prefix:
        q_t_ref, k_ref, v_t_ref, km_ref, vm_t_ref, lam_ref, g_ref, o_ref, m_sc, l_sc, acc_sc, s_sc = refs
    else:
        q_t_ref, k_ref, v_t_ref, lam_ref, g_ref, o_ref, m_sc, l_sc, acc_sc, s_sc = refs
    qi, ki = qi_ref[pl.program_id(1)], ki_ref[pl.program_id(1)]
    hd = 2 * DH_B
    row = lax.broadcasted_iota(jnp.int32, (hd, t), 0)

    def all_maps(k_src, v_t_src, visible):
        n_keys = k_src.shape[0]

        def scores(i, slot):
            h, c = divmod(i, 2)
            n = h // GROUP_B
            q_t = q_t_ref[h]
            q_c = jnp.where((row < DH_B) == (c == 0), q_t, jnp.zeros_like(q_t))
            s_sc[slot, :n_keys] = _dot(k_src[:, n * hd:(n + 1) * hd], q_c)

        def update(i, slot):
            s_t = s_sc[slot, :n_keys]
            if visible is not None:
                s_t = jnp.where(visible, s_t, -jnp.inf)
            _flash_tile(s_t, v_t_src[i // (2 * GROUP_B)], m_sc, l_sc, acc_sc, i)

        n_maps = 2 * H_B
        scores(0, 0)
        for i in range(n_maps):
            if i + 1 < n_maps:
                scores(i + 1, (i + 1) % 2)
            update(i, i % 2)

    @pl.when(ki == 0)
    def _init():
        if has_prefix:
            def score(i):
                h, c = divmod(i, 2)
                n = h // GROUP_B
                q_t = q_t_ref[h]
                q_c = jnp.where((row < DH_B) == (c == 0), q_t, jnp.zeros_like(q_t))
                return _dot(km_ref[:, n * hd:(n + 1) * hd], q_c)

            _flash_prefix(score, lambda i: vm_t_ref[i // (2 * GROUP_B)], 2 * H_B, m_sc, l_sc, acc_sc, s_sc)
        else:
            _flash_reset(m_sc, l_sc, acc_sc)

    @pl.when(ki < qi)
    def _below_diagonal():
        all_maps(k_ref, v_t_ref, None)

    @pl.when(ki == qi)
    def _diagonal():
        all_maps(k_ref, v_t_ref, _causal_visible(t))
        lam = _diff_lambda(lam_ref, lam_init)
        g = g_ref[...] * (1.0 - lam_init)
        outs = []
        for h in range(H_B):
            o = acc_sc[2 * h] / l_sc[2 * h] - lam * (acc_sc[2 * h + 1] / l_sc[2 * h + 1])
            o = o * lax.rsqrt(jnp.mean(o * o, axis=0, keepdims=True) + RMS_EPS)
            outs.append(o * g)
        o_ref[...] = jnp.concatenate(outs, axis=0).T.astype(BF16)


def _diff_attn(q_t, k, v_t, k_meta, v_meta_t, lam_vecs, g_col, lam_init, nb, s, t):
    n = s // t
    has_prefix = k_meta is not None
    hd = 2 * DH_B
    qi_tab, ki_tab = _causal_pairs(n)
    q_blk = lambda b, p, qi, ki: b * n + qi[p]
    k_blk = lambda b, p, qi, ki: b * n + ki[p]
    in_specs = [pl.BlockSpec((H_B, hd, t), lambda *a: (0, 0, q_blk(*a))),
                pl.BlockSpec((t, KVH_B * hd), lambda *a: (k_blk(*a), 0)),
                pl.BlockSpec((KVH_B, hd, t), lambda *a: (0, 0, k_blk(*a)))]
    args = [q_t, k, v_t]
    if has_prefix:
        in_specs += [_full(k_meta.shape), _full(v_meta_t.shape)]
        args += [k_meta, v_meta_t]
    in_specs += [_full(lam_vecs.shape), _full(g_col.shape)]
    args += [lam_vecs, g_col]
    nvh = 2 * H_B
    grid_spec = pltpu.PrefetchScalarGridSpec(
        num_scalar_prefetch=2,
        grid=(nb, qi_tab.shape[0]),
        in_specs=in_specs,
        out_specs=pl.BlockSpec((t, H_B * hd), lambda *a: (q_blk(*a), 0)),
        scratch_shapes=[pltpu.VMEM((nvh, 1, t), F32), pltpu.VMEM((nvh, 1, t), F32),
                        pltpu.VMEM((nvh, hd, t), F32), pltpu.VMEM((2, t, t), F32)],
    )
    return pl.pallas_call(
        functools.partial(_diff_attn_kernel, t=t, has_prefix=has_prefix, lam_init=lam_init),
        grid_spec=grid_spec,
        out_shape=jax.ShapeDtypeStruct((nb * s, H_B * hd), BF16),
        compiler_params=_params(("parallel", "arbitrary"), 48),
        name="diff_attn",
    )(qi_tab, ki_tab, *args)


def _chunk_pipeline(n_chunks_total, start_chunk, wait_chunk, b, per_batch, consume, carry):
    g0 = b * per_batch

    @pl.when(b == 0)
    def _prime():
        for d in range(DEC_NBUF - 1):
            if d < n_chunks_total:
                start_chunk(d, d % DEC_NBUF)

    def body(c, carry):
        g = g0 + c
        slot = g % DEC_NBUF
        wait_chunk(slot)
        nxt = g + DEC_NBUF - 1

        @pl.when(nxt < n_chunks_total)
        def _():
            start_chunk(nxt, nxt % DEC_NBUF)

        return consume(slot, carry)

    return lax.fori_loop(0, per_batch, body, carry)


def _mla_decode_kernel(pt_ref, q_ref, knew_ref, ckv_hbm, kr_hbm, o_ref, ckv_buf, kr_buf, sem_c, sem_r,
                       *, layer, n_pages, cpp, n_batch):
    b = pl.program_id(0)
    per_batch = n_pages // cpp
    total = n_batch * per_batch

    def copies(g, slot, j):
        page = pt_ref[g * cpp + j]
        return (pltpu.make_async_copy(ckv_hbm.at[layer, page], ckv_buf.at[slot, j], sem_c.at[slot]),
                pltpu.make_async_copy(kr_hbm.at[layer, page], kr_buf.at[slot, :, pl.ds(j * PAGE, PAGE)],
                                      sem_r.at[slot]))

    def start_chunk(g, slot):
        for j in range(cpp):
            for cp in copies(g, slot, j):
                cp.start()

    def wait_chunk(slot):
        for j in range(cpp):
            for cp in copies(0, slot, j):
                cp.wait()

    q = q_ref[0]
    qlat, qr = q[:, :KV_LORA], q[:, KV_LORA:KV_LORA + ROPE_A]
    knew = knew_ref[0].astype(F32)
    m0 = jnp.sum(q.astype(F32) * knew, axis=-1, keepdims=True)
    l0 = jnp.ones_like(m0)
    acc0 = jnp.broadcast_to(knew[:, :KV_LORA], (H_A, KV_LORA))

    def consume(slot, carry):
        m_prev, l_prev, acc = carry
        kc = ckv_buf[slot].reshape(cpp * PAGE, KV_LORA).astype(BF16)
        kr_t = kr_buf[slot].astype(BF16)
        s = _dot_nt(qlat, kc) + _dot(qr, kr_t)
        m_new = jnp.maximum(m_prev, jnp.max(s, axis=-1, keepdims=True))
        a = jnp.exp2(m_prev - m_new)
        p = jnp.exp2(s - m_new)
        l_new = a * l_prev + jnp.sum(p, axis=-1, keepdims=True)
        return m_new, l_new, a * acc + _dot(p.astype(BF16), kc)

    _, l, acc = _chunk_pipeline(total, start_chunk, wait_chunk, b, per_batch, consume, (m0, l0, acc0))
    o_ref[0] = acc / l


def _mla_decode(pt_flat, q, knew, cache_ckv, cache_kr_t, layer, n_pages):
    nb = q.shape[0]
    cpp = min(MLA_PAGES_PER_CHUNK, n_pages)
    grid_spec = pltpu.PrefetchScalarGridSpec(
        num_scalar_prefetch=1,
        grid=(nb,),
        in_specs=[pl.BlockSpec((1, H_A, KA_PAD), lambda b, pt: (b, 0, 0)),
                  pl.BlockSpec((1, 1, KA_PAD), lambda b, pt: (b, 0, 0)),
                  pl.BlockSpec(memory_space=pl.ANY), pl.BlockSpec(memory_space=pl.ANY)],
        out_specs=pl.BlockSpec((1, H_A, KV_LORA), lambda b, pt: (b, 0, 0)),
        scratch_shapes=[pltpu.VMEM((DEC_NBUF, cpp, PAGE, KV_LORA), F32),
                        pltpu.VMEM((DEC_NBUF, ROPE_A, cpp * PAGE), F32),
                        pltpu.SemaphoreType.DMA((DEC_NBUF,)), pltpu.SemaphoreType.DMA((DEC_NBUF,))],
    )
    return pl.pallas_call(
        functools.partial(_mla_decode_kernel, layer=layer, n_pages=n_pages, cpp=cpp, n_batch=nb),
        grid_spec=grid_spec,
        out_shape=jax.ShapeDtypeStruct((nb, H_A, KV_LORA), F32),
        compiler_params=_params(("arbitrary",), 32),
        name="mla_decode",
    )(pt_flat, q, knew, cache_ckv, cache_kr_t)


def _mla_uv_kernel(o_ref, wuv_ref, out_ref):
    outs = [_dot(o_ref[h].astype(BF16), wuv_ref[h]) for h in range(H_A)]
    out_ref[...] = jnp.concatenate(outs, axis=1).astype(BF16)


def _mla_uv(olat, wuv):
    m = olat.shape[1]
    return pl.pallas_call(
        _mla_uv_kernel,
        in_specs=[_full(olat.shape), _full(wuv.shape)],
        out_specs=_full((m, H_A * V_A)),
        out_shape=jax.ShapeDtypeStruct((m, H_A * V_A), BF16),
        name="mla_uv",
    )(olat, wuv)


def _diff_decode_kernel(pt_ref, q_ref, knew_ref, vnew_ref, lam_ref, g_ref, k_hbm, v_hbm, o_ref,
                        k_buf, v_buf, sem_k, sem_v, *, layer, n_pages, cpp, n_batch, lam_init):
    b = pl.program_id(0)
    per_batch = n_pages // cpp
    total = n_batch * per_batch
    hd = 2 * DH_B
    prow = PAGE * KVH_B
    rows = 2 * GROUP_B

    def copies(g, slot, j):
        page = pt_ref[g * cpp + j]
        dst = pl.ds(j * prow, prow)
        return (pltpu.make_async_copy(k_hbm.at[layer, page], k_buf.at[slot, dst], sem_k.at[slot]),
                pltpu.make_async_copy(v_hbm.at[layer, page], v_buf.at[slot, dst], sem_v.at[slot]))

    def start_chunk(g, slot):
        for j in range(cpp):
            for cp in copies(g, slot, j):
                cp.start()

    def wait_chunk(slot):
        for j in range(cpp):
            for cp in copies(0, slot, j):
                cp.wait()

    qs = [q_ref[0, n] for n in range(KVH_B)]
    knew = knew_ref[0].astype(F32)
    vnew = vnew_ref[0].astype(F32)
    carry = []
    for n in range(KVH_B):
        m0 = jnp.sum(qs[n].astype(F32) * knew[:, n * hd:(n + 1) * hd], axis=-1, keepdims=True)
        carry += [m0, jnp.ones_like(m0), jnp.broadcast_to(vnew[:, n * hd:(n + 1) * hd], (8, hd))]

    def consume(slot, carry):
        sel = [pl.ds(n, cpp * PAGE, stride=KVH_B) for n in range(KVH_B)]
        scores = [_dot_nt(qs[n], k_buf[slot, sel[n], :].astype(BF16)) for n in range(KVH_B)]
        out, scaled, probs = [], [], []
        for n in range(KVH_B):
            m_prev, l_prev, acc = carry[3 * n:3 * n + 3]
            m_new = jnp.maximum(m_prev, jnp.max(scores[n], axis=-1, keepdims=True))
            a = jnp.exp2(m_prev - m_new)
            p = jnp.exp2(scores[n] - m_new)
            out += [m_new, a * l_prev + jnp.sum(p, axis=-1, keepdims=True), None]
            scaled.append(a * acc)
            probs.append(p.astype(BF16))
        for n in range(KVH_B):
            out[3 * n + 2] = scaled[n] + _dot(probs[n], v_buf[slot, sel[n], :].astype(BF16))
        return tuple(out)

    carry = _chunk_pipeline(total, start_chunk, wait_chunk, b, per_batch, consume, tuple(carry))
    lam = _diff_lambda(lam_ref, lam_init)
    heads = []
    for n in range(KVH_B):
        o = carry[3 * n + 2] / carry[3 * n + 1]
        heads.append(o[0:GROUP_B] - lam * o[GROUP_B:rows])
    o = jnp.concatenate(heads, axis=0)
    o_ref[0] = _rms(o, g_ref[...]) * (1.0 - lam_init)


def _diff_decode(pt_flat, q, knew, vnew, lam_vecs, g_sub, cache_k, cache_v, layer, n_pages, lam_init):
    nb = q.shape[0]
    cpp = min(DIFF_PAGES_PER_CHUNK, n_pages)
    hd = 2 * DH_B
    kw = KVH_B * hd
    grid_spec = pltpu.PrefetchScalarGridSpec(
        num_scalar_prefetch=1,
        grid=(nb,),
        in_specs=[pl.BlockSpec((1, KVH_B, 8, hd), lambda b, pt: (b, 0, 0, 0)),
                  pl.BlockSpec((1, 1, kw), lambda b, pt: (b, 0, 0)),
                  pl.BlockSpec((1, 1, kw), lambda b, pt: (b, 0, 0)),
                  pl.BlockSpec(lam_vecs.shape, lambda b, pt: (0, 0)),
                  pl.BlockSpec(g_sub.shape, lambda b, pt: (0, 0)),
                  pl.BlockSpec(memory_space=pl.ANY), pl.BlockSpec(memory_space=pl.ANY)],
        out_specs=pl.BlockSpec((1, H_B, hd), lambda b, pt: (b, 0, 0)),
        scratch_shapes=[pltpu.VMEM((DEC_NBUF, cpp * PAGE * KVH_B, hd), F32),
                        pltpu.VMEM((DEC_NBUF, cpp * PAGE * KVH_B, hd), F32),
                        pltpu.SemaphoreType.DMA((DEC_NBUF,)), pltpu.SemaphoreType.DMA((DEC_NBUF,))],
    )
    return pl.pallas_call(
        functools.partial(_diff_decode_kernel, layer=layer, n_pages=n_pages, cpp=cpp, n_batch=nb,
                          lam_init=lam_init),
        grid_spec=grid_spec,
        out_shape=jax.ShapeDtypeStruct((nb, H_B, hd), F32),
        compiler_params=_params(("arbitrary",), 40),
        name="diff_decode",
    )(pt_flat, q, knew, vnew, lam_vecs, g_sub, cache_k, cache_v)


def _diff_in_kernel(*refs, with_meta):
    x_ref, wkv_ref, wq_t_ref, wv_t_ref, cos_ref, sin_ref, cos_t_ref, sin_t_ref = refs[:8]
    meta_refs = refs[8:10] if with_meta else None
    k_ref, v_ref, kb_ref, q_t_ref, v_t_ref = refs[-5:]
    xb = x_ref[...].astype(BF16)
    tm = xb.shape[0]
    nk = KVH_B * 2 * DH_B
    hd = 2 * DH_B
    y = _dot(xb, wkv_ref[...])
    cos, sin = cos_ref[...], sin_ref[...]
    parts = []
    for c0 in range(0, nk, LANES):
        sl = y[:, c0:c0 + LANES]
        parts.append(sl * cos + _rot_half(sl, ROT_B // 2) * sin)
    k = jnp.concatenate(parts, axis=1)
    _store_rows([(k_ref, k), (v_ref, y[:, nk:])], meta_refs)
    kb_ref[...] = k.astype(BF16)
    v_t_ref[...] = _dot_nt(wv_t_ref[...], xb).reshape(KVH_B, hd, tm).astype(BF16)
    q_t = _dot_nt(wq_t_ref[...], xb)
    cos_t, sin_t = cos_t_ref[...], sin_t_ref[...]
    half = ROT_B // 2
    pieces = []
    for b0 in range(0, H_B * hd, DH_B):
        r1, r2 = _rope_rows(q_t[b0:b0 + half], q_t[b0 + half:b0 + 2 * half], cos_t, sin_t)
        pieces += [r1, r2, q_t[b0 + 2 * half:b0 + DH_B]]
    q_t_ref[...] = (jnp.concatenate(pieces, axis=0) * SCALE_B).reshape(H_B, hd, tm).astype(BF16)
    _store_meta_rows((k_ref, v_ref), meta_refs)


def _diff_in(x, wkv, wq_t, wv_t, cos, sin, cos_t, sin_t, metas=None):
    m, d = x.shape
    tm = min(TM, m)
    nt = cos.shape[0] // tm
    nk = KVH_B * 2 * DH_B
    hd = 2 * DH_B
    row = lambda b, j: (b * nt + j, 0)
    col = lambda b, j: (0, 0, b * nt + j)
    tab = lambda b, j: (j, 0)
    tab_t = lambda b, j: (0, j)
    f32_specs, f32_shapes = _row_specs(m, tm, nt, (nk, nk), metas, split=KVH_B)
    extra = [] if metas is None else list(metas)
    return pl.pallas_call(
        functools.partial(_diff_in_kernel, with_meta=metas is not None),
        grid=(m // (nt * tm), nt),
        in_specs=[pl.BlockSpec((tm, d), row), _resident(wkv.shape), _resident(wq_t.shape), _resident(wv_t.shape),
                  pl.BlockSpec((tm, LANES), tab), pl.BlockSpec((tm, LANES), tab),
                  pl.BlockSpec((ROT_B // 2, tm), tab_t), pl.BlockSpec((ROT_B // 2, tm), tab_t)]
                 + [_full(a.shape) for a in extra],
        out_specs=f32_specs + [pl.BlockSpec((tm, nk), row), pl.BlockSpec((H_B, hd, tm), col),
                               pl.BlockSpec((KVH_B, hd, tm), col)],
        out_shape=f32_shapes + [jax.ShapeDtypeStruct((m, nk), BF16), jax.ShapeDtypeStruct((H_B, hd, m), BF16),
                                jax.ShapeDtypeStruct((KVH_B, hd, m), BF16)],
        compiler_params=_params(("parallel", "arbitrary"), 52),
        name="diff_in",
    )(x, wkv, wq_t, wv_t, cos, sin, cos_t, sin_t, *extra)


def _proj_ln_kernel(o_ref, w_ref, x_ref, g_ref, b_ref, out_ref, *, alpha):
    mix = _dot(o_ref[...], w_ref[...])
    out_ref[...] = _layernorm(alpha * x_ref[...] + mix, g_ref[...], b_ref[...])


def _proj_ln(o, w, x, g, b, alpha):
    m, d = x.shape
    tm = min(TM, m)
    row = lambda i: (i, 0)
    return pl.pallas_call(
        functools.partial(_proj_ln_kernel, alpha=alpha),
        grid=(m // tm,),
        in_specs=[pl.BlockSpec((tm, o.shape[1]), row), _full(w.shape), pl.BlockSpec((tm, d), row),
                  _full(g.shape), _full(b.shape)],
        out_specs=pl.BlockSpec((tm, d), row),
        out_shape=jax.ShapeDtypeStruct((m, d), F32),
        compiler_params=_params(("parallel",), 32),
        name="proj_ln",
    )(o, w, x, g, b)


def _ffn_kernel(x_ref, wg_ref, wu_ref, wd_ref, g_ref, b_ref, out_ref, acc_ref, *, alpha, d_ff):
    x = x_ref[...]
    xb = x.astype(BF16)
    for i, c0 in enumerate(range(0, d_ff, FF_CHUNK)):
        gate = _dot(xb, wg_ref[:, c0:c0 + FF_CHUNK])
        up = _dot(xb, wu_ref[:, c0:c0 + FF_CHUNK])
        act = (gate * jax.nn.sigmoid(gate) * up).astype(BF16)
        part = _dot(act, wd_ref[c0:c0 + FF_CHUNK, :])
        if i == 0:
            acc_ref[...] = part
        else:
            acc_ref[...] += part
    out_ref[...] = _layernorm(alpha * x + acc_ref[...], g_ref[...], b_ref[...])


def _ffn(x, wg, wu, wd, g, b, alpha):
    m, d = x.shape
    d_ff = wg.shape[1]
    tm = min(TM, m)
    row = lambda i: (i, 0)
    return pl.pallas_call(
        functools.partial(_ffn_kernel, alpha=alpha, d_ff=d_ff),
        grid=(m // tm,),
        in_specs=[pl.BlockSpec((tm, d), row), _resident(wg.shape), _resident(wu.shape), _resident(wd.shape),
                  _full(g.shape), _full(b.shape)],
        out_specs=pl.BlockSpec((tm, d), row),
        out_shape=jax.ShapeDtypeStruct((m, d), F32),
        scratch_shapes=[pltpu.VMEM((tm, d), F32)],
        compiler_params=_params(("parallel",), 48),
        name="ffn",
    )(x, wg, wu, wd, g, b)


def _rope_angles(pos, d):
    inv = jnp.power(ROPE_THETA, -jnp.arange(0, d, 2, dtype=F32) / d)
    ang = pos.astype(F32)[:, None] * inv[None, :]
    return jnp.cos(ang), jnp.sin(ang)


def _rope_tables(pos, d, period):
    cos, sin = _rope_angles(pos, d)
    n = pos.shape[0]
    cos_row = jnp.concatenate([cos, cos, jnp.ones((n, period - d), F32)], axis=1)
    sin_row = jnp.concatenate([sin, sin, jnp.zeros((n, period - d), F32)], axis=1)
    reps = LANES // period
    return jnp.tile(cos_row, (1, reps)), jnp.tile(sin_row, (1, reps)), cos.T, sin.T


def kernel(x_prompt, x_sample, cache_mla_ckv, cache_mla_krope, cache_diff_k, cache_diff_v, page_table, meta_tokens, mla_w_in, mla_g_q, mla_g_kv, mla_w_uq, mla_w_uk, mla_w_uv, mla_w_o, diff_w_in, diff_lam_q1, diff_lam_k1, diff_lam_q2, diff_lam_k2, diff_g_sub, diff_w_o, ffn_w_gu, ffn_w_down, ln1_g, ln1_b, ln2_g, ln2_b):
    nb, seq, d = x_prompt.shape
    ndec, dec_seq, _ = x_sample.shape
    assert dec_seq == 1 and meta_tokens.shape[0] == N_META
    n_small = N_META + ndec
    assert n_small <= SMALL_PAD and seq % min(ATT_T, seq) == 0
    n_pages = page_table.shape[1]
    past = n_pages * PAGE
    depth = ln1_g.shape[0]
    alpha = (2 * depth) ** 0.25
    d_ff = ffn_w_down.shape[1]
    hd = 2 * DH_B
    t_att = min(ATT_T, seq)

    hr = x_prompt.reshape(nb * seq, d)
    hs = jnp.concatenate([meta_tokens.astype(F32), x_sample.reshape(ndec, d)], axis=0)
    pos_r = N_META + jnp.arange(seq, dtype=jnp.int32)
    pos_s = jnp.concatenate([jnp.arange(N_META, dtype=jnp.int32), jnp.full((SMALL_PAD - N_META,), past, jnp.int32)])
    pt_flat = page_table.reshape(-1).astype(jnp.int32)
    row2 = lambda a: a.reshape(1, -1).astype(F32)
    pad_small = lambda a: jnp.pad(a, ((0, SMALL_PAD - n_small), (0, 0)))
    dec = slice(N_META, n_small)

    outs = {k: [] for k in ("p_ckv", "p_kr", "p_k", "p_v", "s_ckv", "s_kr", "s_k", "s_v")}

    for i in range(depth):
        j = i // 2
        if i % 2 == 0:
            tr = _rope_tables(pos_r, ROPE_A, ROPE_A)
            ts = _rope_tables(pos_s, ROPE_A, ROPE_A)
            w_in = jnp.pad(mla_w_in[j], ((0, 0), (0, LANES - ROPE_A))).astype(BF16)
            wq = mla_w_uq[j].reshape(Q_LORA, H_A, NOPE_A + ROPE_A)
            wuq = jnp.concatenate([wq[:, :, :NOPE_A].reshape(Q_LORA, H_A * NOPE_A),
                                   wq[:, :, NOPE_A:].reshape(Q_LORA, H_A * ROPE_A)], axis=1).astype(BF16)
            wuk_t = jnp.swapaxes(mla_w_uk[j], 1, 2).astype(BF16)
            wuv = mla_w_uv[j].astype(BF16)
            wk = jnp.pad(mla_w_uk[j], ((0, 0), (0, 0), (0, LANES - NOPE_A)))
            wk = jnp.swapaxes(wk, 0, 1).reshape(KV_LORA, H_A * LANES).astype(BF16)
            wv_t = jnp.swapaxes(mla_w_uv[j], 1, 2).reshape(H_A * V_A, KV_LORA).astype(BF16)
            wq_t = jnp.pad(wq, ((0, 0), (0, 0), (0, LANES - NOPE_A - ROPE_A)))
            wq_t = jnp.transpose(wq_t, (1, 2, 0)).reshape(H_A * LANES, Q_LORA).astype(BF16)
            gq, gkv = row2(mla_g_q[j]), row2(mla_g_kv[j])

            _, _, k_s, vt_s, qt_s = _mla_proj(pad_small(hs), w_in, gq, gkv, *ts, wk, wv_t, wq_t)
            cq_s, ckv_s, kr_s, kb_s = _mla_in(hs, w_in, gq, gkv, ts[0][:n_small], ts[1][:n_small])
            ckv_r, kr_r, k_r, vt_r, qt_r = _mla_proj(hr, w_in, gq, gkv, *tr, wk, wv_t, wq_t,
                                                     metas=(ckv_s[:N_META], kr_s[:N_META]))
            q_s = _mla_q(cq_s, wuq, wuk_t, ts[0][:n_small], ts[1][:n_small])

            k_m, vt_m, qt_m = k_s[:, :LANES], vt_s[:, :, :LANES], qt_s[:, :, :LANES]
            o_r = _mla_attn(qt_r, k_r, vt_r, k_s[:, :N_META], vt_s[:, :, :N_META], nb, seq, t_att)
            o_m = _mla_attn(qt_m, k_m, vt_m, None, None, 1, LANES, LANES)[:N_META]
            olat = _mla_decode(pt_flat, jnp.swapaxes(q_s[:, dec], 0, 1), kb_s[dec].reshape(ndec, 1, KA_PAD),
                               cache_mla_ckv, jnp.swapaxes(cache_mla_krope, 2, 3), j, n_pages)
            o_d = _mla_uv(jnp.swapaxes(olat, 0, 1), wuv)
            w_o = mla_w_o[j].astype(BF16)

            outs["p_ckv"].append(ckv_r)
            outs["p_kr"].append(kr_r)
            outs["s_ckv"].append(ckv_s[dec].reshape(ndec, 1, KV_LORA))
            outs["s_kr"].append(kr_s[dec].reshape(ndec, 1, ROPE_A))
        else:
            lam_init = 0.8 - 0.6 * math.exp(-0.3 * i)
            tr = _rope_tables(pos_r, ROT_B, DH_B)
            ts = _rope_tables(pos_s, ROT_B, DH_B)
            nq, nk = H_B * hd, KVH_B * hd
            w = diff_w_in[j]
            wkv = w[:, nq:].astype(BF16)
            wq_t = w[:, :nq].T.astype(BF16)
            wv_t = w[:, nq + nk:].T.astype(BF16)
            lam_vecs = jnp.stack([diff_lam_q1[j], diff_lam_k1[j], diff_lam_q2[j], diff_lam_k2[j]]).astype(F32)
            g_sub = row2(diff_g_sub[j])

            k_s, v_s, kb_s, qt_s, vt_s = _diff_in(pad_small(hs), wkv, wq_t, wv_t, *ts)
            meta_kv = (k_s[:N_META].reshape(N_META * KVH_B, hd), v_s[:N_META].reshape(N_META * KVH_B, hd))
            k_r, v_r, kb_r, qt_r, vt_r = _diff_in(hr, wkv, wq_t, wv_t, *tr, metas=meta_kv)

            k_m, vt_m, qt_m = kb_s[:LANES], vt_s[:, :, :LANES], qt_s[:, :, :LANES]
            o_r = _diff_attn(qt_r, kb_r, vt_r, kb_s[:N_META], vt_s[:, :, :N_META], lam_vecs, g_sub.T, lam_init,
                             nb, seq, t_att)
            o_m = _diff_attn(qt_m, k_m, vt_m, None, None, lam_vecs, g_sub.T, lam_init, 1, LANES, LANES)[:N_META]
            qd = jnp.transpose(qt_s[:, :, dec], (2, 0, 1)).reshape(ndec, KVH_B, GROUP_B, 2, DH_B)
            qd = jnp.swapaxes(qd, 2, 3)
            eye = jnp.eye(2, dtype=BF16)
            qd = (qd[:, :, :, :, None, :] * eye[None, None, :, None, :, None]).reshape(ndec, KVH_B, 2 * GROUP_B, hd)
            qd = jnp.pad(qd, ((0, 0), (0, 0), (0, 8 - 2 * GROUP_B), (0, 0)))
            n_pool = cache_diff_k.shape[1]
            ck = cache_diff_k.reshape(cache_diff_k.shape[0], n_pool, PAGE * KVH_B, hd)
            cv = cache_diff_v.reshape(cache_diff_v.shape[0], n_pool, PAGE * KVH_B, hd)
            o_d = _diff_decode(pt_flat, qd, kb_s[dec].reshape(ndec, 1, nk),
                               v_s[dec].astype(BF16).reshape(ndec, 1, nk), lam_vecs, g_sub, ck, cv, j, n_pages,
                               lam_init)
            o_d = o_d.reshape(ndec, H_B * hd).astype(BF16)
            w_o = diff_w_o[j].astype(BF16)

            outs["p_k"].append(k_r.reshape(nb, N_META + seq, KVH_B, hd))
            outs["p_v"].append(v_r.reshape(nb, N_META + seq, KVH_B, hd))
            outs["s_k"].append(k_s[dec].reshape(ndec, 1, KVH_B, hd))
            outs["s_v"].append(v_s[dec].reshape(ndec, 1, KVH_B, hd))

        o_s = jnp.concatenate([o_m, o_d], axis=0)
        g1, b1, g2, b2 = row2(ln1_g[i]), row2(ln1_b[i]), row2(ln2_g[i]), row2(ln2_b[i])
        hr = _proj_ln(o_r, w_o, hr, g1, b1, alpha)
        hs = _proj_ln(o_s, w_o, hs, g1, b1, alpha)
        wg = ffn_w_gu[i][:, :d_ff].astype(BF16)
        wu = ffn_w_gu[i][:, d_ff:].astype(BF16)
        wd = ffn_w_down[i].astype(BF16)
        hr = _ffn(hr, wg, wu, wd, g2, b2, alpha)
        hs = _ffn(hs, wg, wu, wd, g2, b2, alpha)

    y_prompt = hr.reshape(nb, seq, d)
    y_sample = hs[dec].reshape(ndec, 1, d)
    st = lambda k: outs[k][0][None] if len(outs[k]) == 1 else jnp.stack(outs[k], axis=0)
    return (y_prompt, y_sample, st("p_ckv"), st("p_kr"), st("p_k"), st("p_v"),
            st("s_ckv"), st("s_kr"), st("s_k"), st("s_v"))
```
